```python
import math
import jax, jax.numpy as jnp
from jax import lax
import numpy as np

D_MODEL = 1024
BATCH = 16
SEQ = 256
DEPTH = 4
DEC_BATCH = 2
DEC_SEQ = 2048
PAST_LEN = 256

GRID_W = 64
BR_W = 512
HY_W = BR_W
HY_EMB = 33
HY_BANDS = (HY_EMB - 1) // 2
HY_FFN = 64
HY_MAX_DECAY = math.log(1e-2) / 0.3
HY_MIN_DECAY = math.log(1e-2) / 1.5
RW_N = 64
RW_H = BR_W // RW_N
RW_W = RW_H * RW_N
W_LORA = 64
A_LORA = 64
RW_DECAY_SCALE = math.exp(-0.5)
RW_GN_EPS = 64e-5
DA_DH = 64
DA_H = BR_W // (2 * DA_DH)
DA_QW = DA_H * 2 * DA_DH
DA_VW = DA_H * 2 * DA_DH
ROPE_AXIS = DA_DH // 2
ROPE_BASE = 10000.0
Q_BLOCK = 128
NORM_EPS = 1e-6
N_BRANCH = 3
RW_SHIFT_W = 3 * RW_W + 2 * W_LORA + 2 * A_LORA
IN_SIZES = (3 * HY_W, HY_W, RW_SHIFT_W, RW_W, DA_QW, DA_QW, DA_VW, DA_VW, N_BRANCH * D_MODEL)
N_IN = sum(IN_SIZES)
IN_SPLITS = tuple(int(s) for s in np.cumsum(IN_SIZES)[:-1])

kernel_name = "hyena_rwkv7_diffattn_prefix_dit"

F32 = jnp.float32


def _rmsnorm(x, g, eps=NORM_EPS):
    xf = x.astype(F32)
    y = xf * lax.rsqrt(jnp.mean(xf * xf, axis=-1, keepdims=True) + eps)
    return (y * g.astype(F32)).astype(x.dtype)


def _centred_conv3(u, w, b):
    up = jnp.pad(u, ((0, 0), (1, 1), (0, 0)))
    return up[:, :-2] * w[0] + up[:, 1:-1] * w[1] + up[:, 2:] * w[2] + b


def _hyena_filters(L, f1, fb1, freq, f2, fb2, f3):
    t = jnp.linspace(0.0, 1.0, L, dtype=F32)[:, None]
    w = 2.0 * math.pi * jnp.arange(L, dtype=F32)[:, None] / L
    bands = jnp.linspace(1e-4, HY_BANDS - 1, HY_BANDS, dtype=F32)[None, :]
    z = jnp.concatenate([t, jnp.cos(bands * w), -jnp.sin(bands * w)], axis=-1)
    freq = freq.astype(F32)
    hid = jnp.sin(freq * (z @ f1.astype(F32) + fb1.astype(F32)))
    hid = jnp.sin(freq * (hid @ f2.astype(F32) + fb2.astype(F32)))
    h = hid @ f3.astype(F32)
    deltas = jnp.linspace(HY_MIN_DECAY, HY_MAX_DECAY, HY_W, dtype=F32)
    window = jnp.exp(-t * jnp.abs(deltas)[None, :])
    return h[:, :HY_W] * window, h[:, HY_W:] * window


def _long_conv_bidir(u, h_f, h_b, bias_d):
    L = u.shape[1]
    C = u.shape[2]
    h_full = jnp.concatenate([h_f, jnp.zeros((1, C), F32), h_b[1:][::-1]], axis=0)
    uf = u.astype(F32)
    U = jnp.fft.rfft(uf, n=2 * L, axis=1)
    H = jnp.fft.rfft(h_full, n=2 * L, axis=0)
    y = jnp.fft.irfft(U * H[None], n=2 * L, axis=1)[:, :L]
    return y + uf * bias_d.astype(F32)


def _hyena_branch(z, gate, p):
    u = _centred_conv3(z, p["hy_conv_w"], p["hy_conv_b"])
    v, x1, x2 = jnp.split(u, 3, axis=-1)
    h_f, h_b = _hyena_filters(u.shape[1], p["hy_f1"], p["hy_fb1"], p["hy_freq"],
                              p["hy_f2"], p["hy_fb2"], p["hy_f3"])
    y = x1.astype(F32) * _long_conv_bidir(x2 * v, h_f, h_b, p["hy_bias"])
    return (y * jax.nn.silu(gate.astype(F32))).astype(z.dtype)


def _rwkv_scan(r, w, kk, a, k, v, s0, reverse):
    def step(S, inp):
        r_t, w_t, kk_t, a_t, k_t, v_t = inp
        sa = jnp.einsum('bhvk,bhk->bhv', S, -kk_t)
        S = (S * w_t[:, :, None, :] + sa[..., None] * (kk_t * a_t)[:, :, None, :]
             + v_t[..., None] * k_t[:, :, None, :])
        return S, jnp.einsum('bhvk,bhk->bhv', S, r_t)
    xs = tuple(jnp.moveaxis(t, 1, 0) for t in (r, w, kk, a, k, v))
    S, ys = lax.scan(step, s0.astype(F32), xs, reverse=reverse)
    return jnp.moveaxis(ys, 0, 1), S


def _rwkv_branch(zr, gate, p, s0_f, s0_b):
    B, L, _ = zr.shape
    zr = zr.astype(F32)
    zp = jnp.pad(zr, ((0, 0), (1, 1), (0, 0)))
    nbr = 0.5 * (zp[:, :-2] + zp[:, 2:])
    zr = zr + p["rw_mu"].astype(F32) * (nbr - zr)
    r, k, v, wl, al = jnp.split(zr, [RW_W, 2 * RW_W, 3 * RW_W, 3 * RW_W + 2 * W_LORA], axis=-1)
    wls = jnp.split(wl, 2, axis=-1)
    als = jnp.split(al, 2, axis=-1)
    heads = lambda t: t.reshape(B, L, RW_H, RW_N)
    kk = heads(k * p["rw_kk"].astype(F32))
    kk = kk * lax.rsqrt(jnp.sum(kk * kk, axis=-1, keepdims=True) + 1e-12)
    ys, states, kds = [], [], []
    for d, (s0, rev) in enumerate(((s0_f, False), (s0_b, True))):
        w = jnp.exp(-RW_DECAY_SCALE * jax.nn.sigmoid(
            p["rw_w0"][d].astype(F32) + jnp.tanh(wls[d]) @ p["rw_w2"][d].astype(F32)))
        a = jax.nn.sigmoid(p["rw_a0"][d].astype(F32) + als[d] @ p["rw_a2"][d].astype(F32))
        kd = k * (1.0 + (a - 1.0) * p["rw_ka"].astype(F32))
        y, s = _rwkv_scan(heads(r), heads(w), kk, heads(a), heads(kd), heads(v), s0, rev)
        ys.append(y)
        states.append(s)
        kds.append(kd)
    y = ys[0] + ys[1]
    mu = jnp.mean(y, axis=-1, keepdims=True)
    var = jnp.mean(jnp.square(y - mu), axis=-1, keepdims=True)
    y = ((y - mu) * lax.rsqrt(var + RW_GN_EPS)).reshape(B, L, RW_W)
    y = y * p["rw_ln_w"].astype(F32) + p["rw_ln_b"].astype(F32)
    k_mix = heads(0.5 * (kds[0] + kds[1]))
    bonus = jnp.sum(heads(r) * k_mix * p["rw_rk"].astype(F32), axis=-1, keepdims=True) * heads(v)
    y = (y + bonus.reshape(B, L, RW_W)) * jax.nn.silu(gate.astype(F32))
    return y.astype(gate.dtype), states[0], states[1]


def _rope_axis(x, pos):
    half = ROPE_AXIS // 2
    freqs = ROPE_BASE ** (-jnp.arange(half, dtype=F32) / half)
    ang = pos.astype(F32)[:, None] * freqs[None, :]
    cos = jnp.cos(ang)[None, :, None, None, :]
    sin = jnp.sin(ang)[None, :, None, None, :]
    x1, x2 = x[..., :half], x[..., half:]
    return jnp.concatenate([x1 * cos - x2 * sin, x1 * sin + x2 * cos], axis=-1)


def _rope2d(x):
    L = x.shape[1]
    rows = L // GRID_W
    row = jnp.repeat(jnp.arange(rows), GRID_W)
    col = jnp.tile(jnp.arange(GRID_W), rows)
    xf = x.astype(F32)
    out = jnp.concatenate([_rope_axis(xf[..., :ROPE_AXIS], row),
                           _rope_axis(xf[..., ROPE_AXIS:], col)], axis=-1)
    return out.astype(x.dtype)


def _diff_attention(q, k, v, lam):
    B, Lq, H, _, Dh = q.shape
    nb = Lq // Q_BLOCK
    qb = jnp.moveaxis(q.reshape(B, nb, Q_BLOCK, H, 2, Dh), 1, 0)
    kf = k.astype(F32)
    vf = v.astype(F32)
    scale = Dh ** -0.5

    def block(qi):
        s = jnp.einsum('bqhmd,bkhmd->bhmqk', qi.astype(F32), kf) * scale
        pr = jax.nn.softmax(s, axis=-1)
        pd = pr[:, :, 0] - lam * pr[:, :, 1]
        return jnp.einsum('bhqk,bkhe->bqhe', pd, vf)
    o = lax.map(block, qb)
    return jnp.moveaxis(o, 0, 1).reshape(B, Lq, H, 2 * Dh)


def _diff_branch(q, k, v, gate, p, lam_init, ctx_kv):
    B, L, _ = q.shape
    q = _rmsnorm(q.reshape(B, L, DA_H, 2, DA_DH), p["da_gq"])
    k = _rmsnorm(k.reshape(B, L, DA_H, 2, DA_DH), p["da_gk"])
    v = v.reshape(B, L, DA_H, 2 * DA_DH)
    if ctx_kv is None:
        keys, vals = k, v
    else:
        q = _rope2d(q)
        keys = jnp.concatenate([ctx_kv[0].astype(k.dtype), _rope2d(k)], axis=1)
        vals = jnp.concatenate([ctx_kv[1].astype(v.dtype), v], axis=1)
    lam = (jnp.exp(jnp.sum(p["da_lq1"].astype(F32) * p["da_lk1"].astype(F32)))
           - jnp.exp(jnp.sum(p["da_lq2"].astype(F32) * p["da_lk2"].astype(F32))) + lam_init)
    o = _diff_attention(q, keys, vals, lam)
    o = _rmsnorm(o, p["da_gsub"], 1e-5) * (1.0 - lam_init)
    y = o.reshape(B, L, DA_VW) * jax.nn.silu(gate.astype(F32))
    return y.astype(gate.dtype), k, v


def _layer(x, cvec, p, lam_init, ctx):
    mod = jax.nn.silu(cvec) @ p["w_ada"] + p["b_ada"]
    shift, scale, gate = jnp.split(mod, 3, axis=-1)
    h = _rmsnorm(x, p["norm_g"]) * (1.0 + scale[:, None, :]) + shift[:, None, :]
    z = h @ p["w_in"]
    hy_z, hy_g, rw_z, rw_g, da_q, da_k, da_v, da_g, mg = jnp.split(z, IN_SPLITS, axis=-1)
    if ctx is None:
        s0 = jnp.zeros((x.shape[0], RW_H, RW_N, RW_N), F32)
        s0_f, s0_b, ctx_kv = s0, s0, None
    else:
        ctx_k, ctx_v, s0_f, s0_b = ctx
        ctx_kv = (ctx_k, ctx_v)
    y_a = _hyena_branch(hy_z, hy_g, p)
    y_b, s_f, s_b = _rwkv_branch(rw_z, rw_g, p, s0_f, s0_b)
    y_c, k_c, v_c = _diff_branch(da_q, da_k, da_v, da_g, p, lam_init, ctx_kv)
    g_a, g_b, g_c = jnp.split(jax.nn.sigmoid(mg.astype(F32)), 3, axis=-1)
    merged = (g_a * (y_a @ p["w_br"][0]) + g_b * (y_b @ p["w_br"][1])
              + g_c * (y_c @ p["w_br"][2]))
    out = merged.astype(x.dtype) @ p["w_out"]
    x_new = x + (gate[:, None, :] * out).astype(x.dtype)
    if ctx is None:
        return x_new, (k_c, v_c, jnp.stack([s_f, s_b], axis=1))
    return x_new, None


def setup_inputs(seed: int = 0) -> dict:
    key = jax.random.key(seed)
    ks = iter(jax.random.split(key, 48))
    nrm = lambda shape, s=1.0: jax.random.normal(next(ks), shape, F32) * s
    D = D_MODEL
    return {
        "x_prompt": nrm((BATCH, SEQ, D)),
        "x_sample": nrm((DEC_BATCH, DEC_SEQ, D)),
        "cache_k": nrm((DEC_BATCH, DEPTH, PAST_LEN, DA_H, 2, DA_DH)),
        "cache_v": nrm((DEC_BATCH, DEPTH, PAST_LEN, DA_H, 2 * DA_DH)),
        "state_rwkv": nrm((DEC_BATCH, DEPTH, 2, RW_H, RW_N, RW_N), 0.5),
        "c": nrm((DEC_BATCH, D)),
        "c_ctx": nrm((D,)),
        "norm_g": 1.0 + nrm((DEPTH, D), 0.05),
        "w_ada": nrm((DEPTH, D, 3 * D), 0.02),
        "b_ada": nrm((DEPTH, 3 * D), 0.01),
        "w_in": nrm((DEPTH, D, N_IN), D ** -0.5),
        "hy_conv_w": nrm((DEPTH, 3, 3 * HY_W), 3 ** -0.5),
        "hy_conv_b": nrm((DEPTH, 3 * HY_W), 0.01),
        "hy_f1": nrm((DEPTH, HY_EMB, HY_FFN), HY_EMB ** -0.5),
        "hy_fb1": nrm((DEPTH, HY_FFN), 0.1),
        "hy_freq": 1.0 + nrm((DEPTH, HY_FFN), 0.05),
        "hy_f2": nrm((DEPTH, HY_FFN, HY_FFN), HY_FFN ** -0.5),
        "hy_fb2": nrm((DEPTH, HY_FFN), 0.1),
        "hy_f3": nrm((DEPTH, HY_FFN, 2 * HY_W), 0.05 * HY_FFN ** -0.5),
        "hy_bias": nrm((DEPTH, HY_W), 0.1),
        "rw_mu": jax.random.uniform(next(ks), (DEPTH, RW_SHIFT_W), F32),
        "rw_w0": nrm((DEPTH, 2, RW_W), 0.5),
        "rw_w2": nrm((DEPTH, 2, W_LORA, RW_W), W_LORA ** -0.5),
        "rw_a0": nrm((DEPTH, 2, RW_W), 0.5),
        "rw_a2": nrm((DEPTH, 2, A_LORA, RW_W), 0.5 * A_LORA ** -0.5),
        "rw_kk": 0.85 + nrm((DEPTH, RW_W), 0.05),
        "rw_ka": 1.0 + nrm((DEPTH, RW_W), 0.05),
        "rw_rk": nrm((DEPTH, RW_H, RW_N), 0.1),
        "rw_ln_w": 1.0 + nrm((DEPTH, RW_W), 0.05),
        "rw_ln_b": nrm((DEPTH, RW_W), 0.01),
        "da_gq": 1.0 + nrm((DEPTH, DA_DH), 0.05),
        "da_gk": 1.0 + nrm((DEPTH, DA_DH), 0.05),
        "da_lq1": nrm((DEPTH, DA_DH), 0.1),
        "da_lk1": nrm((DEPTH, DA_DH), 0.1),
        "da_lq2": nrm((DEPTH, DA_DH), 0.1),
        "da_lk2": nrm((DEPTH, DA_DH), 0.1),
        "da_gsub": 1.0 + nrm((DEPTH, 2 * DA_DH), 0.05),
        "w_br": nrm((DEPTH, N_BRANCH, BR_W, D), BR_W ** -0.5),
        "w_out": nrm((DEPTH, D, D), D ** -0.5),
    }


def reference(x_prompt, x_sample, cache_k, cache_v, state_rwkv, c, c_ctx,
              norm_g, w_ada, b_ada, w_in, hy_conv_w, hy_conv_b, hy_f1, hy_fb1, hy_freq,
              hy_f2, hy_fb2, hy_f3, hy_bias, rw_mu, rw_w0, rw_w2, rw_a0, rw_a2, rw_kk, rw_ka,
              rw_rk, rw_ln_w, rw_ln_b, da_gq, da_gk, da_lq1, da_lk1, da_lq2, da_lk2, da_gsub,
              w_br, w_out):
    def layer_params(l):
        return dict(norm_g=norm_g[l], w_ada=w_ada[l], b_ada=b_ada[l], w_in=w_in[l],
                    hy_conv_w=hy_conv_w[l], hy_conv_b=hy_conv_b[l], hy_f1=hy_f1[l], hy_fb1=hy_fb1[l],
                    hy_freq=hy_freq[l], hy_f2=hy_f2[l], hy_fb2=hy_fb2[l], hy_f3=hy_f3[l],
                    hy_bias=hy_bias[l], rw_mu=rw_mu[l], rw_w0=rw_w0[l], rw_w2=rw_w2[l],
                    rw_a0=rw_a0[l], rw_a2=rw_a2[l], rw_kk=rw_kk[l], rw_ka=rw_ka[l], rw_rk=rw_rk[l],
                    rw_ln_w=rw_ln_w[l], rw_ln_b=rw_ln_b[l], da_gq=da_gq[l], da_gk=da_gk[l],
                    da_lq1=da_lq1[l], da_lk1=da_lk1[l], da_lq2=da_lq2[l], da_lk2=da_lk2[l],
                    da_gsub=da_gsub[l], w_br=w_br[l], w_out=w_out[l])

    c_prompt = jnp.broadcast_to(c_ctx, (x_prompt.shape[0], D_MODEL))
    xp = x_prompt
    new_k, new_v, new_s = [], [], []
    for l in range(DEPTH):
        lam_init = 0.8 - 0.6 * math.exp(-0.3 * l)
        xp, (k_l, v_l, s_l) = _layer(xp, c_prompt, layer_params(l), lam_init, None)
        new_k.append(k_l)
        new_v.append(v_l)
        new_s.append(s_l)
    new_cache_k = jnp.stack(new_k, axis=1)
    new_cache_v = jnp.stack(new_v, axis=1)
    new_state_rwkv = jnp.stack(new_s, axis=1)

    xs = x_sample
    for l in range(DEPTH):
        lam_init = 0.8 - 0.6 * math.exp(-0.3 * l)
        ctx = (cache_k[:, l], cache_v[:, l], state_rwkv[:, l, 0], state_rwkv[:, l, 1])
        xs, _ = _layer(xs, c, layer_params(l), lam_init, ctx)
    return (xp, xs, new_cache_k, new_cache_v, new_state_rwkv)
```

```python
import functools
import math

import jax
import jax.numpy as jnp
from jax import lax
from jax.experimental import pallas as pl
from jax.experimental.pallas import tpu as pltpu

F32 = jnp.float32
BF16 = jnp.bfloat16

D_MODEL = 1024
DEPTH = 4
GRID_W = 64
BR_W = 512
HY_EMB = 33
HY_BANDS = (HY_EMB - 1) // 2
HY_FFN = 64
HY_MAX_DECAY = math.log(1e-2) / 0.3
HY_MIN_DECAY = math.log(1e-2) / 1.5
RW_N = 64
RW_H = BR_W // RW_N
W_LORA = 64
A_LORA = 64
RW_DECAY_SCALE = math.exp(-0.5)
RW_GN_EPS = 64e-5
DA_DH = 64
DA_H = BR_W // (2 * DA_DH)
ROPE_AXIS = DA_DH // 2
ROPE_BASE = 10000.0
NORM_EPS = 1e-6
N_IN = 9472

LANES = 128
ROW_TILE = 256
HALO = 8
SCAN_CHUNK = 128
SUB = 64
VMEM_LIMIT = 52 * 1024 * 1024

COL_HY_Z = 0
COL_HY_G = 1536
COL_RW_R = 2048
COL_RW_K = 2560
COL_RW_V = 3072
COL_RW_G = 3584
COL_DA_Q = 4096
COL_DA_K = 4608
COL_DA_V = 5120
COL_DA_G = 5632
COL_MG = 6144
COL_LORA = 9216


def _sigmoid(x):
    return 1.0 / (1.0 + jnp.exp(-x))


def _silu(x):
    return x * _sigmoid(x)


def _dot(a, b):
    return jnp.dot(a, b, preferred_element_type=F32)


def _split2(x):
    hi = x.astype(BF16)
    lo = (x - hi.astype(F32)).astype(BF16)
    return hi, lo


def _split3(x):
    hi = x.astype(BF16)
    r1 = x - hi.astype(F32)
    mid = r1.astype(BF16)
    lo = (r1 - mid.astype(F32)).astype(BF16)
    return hi, mid, lo


def _dot3(a, b):
    ah, al = _split2(a)
    bh, bl = _split2(b)
    return _dot(ah, bh) + _dot(ah, bl) + _dot(al, bh)


def _dot3s(ah, al, bh, bl):
    return _dot(ah, bh) + _dot(ah, bl) + _dot(al, bh)


def _segsum(x, ones_bd):
    hi, mid, lo = _split3(x)
    return _dot(hi, ones_bd) + _dot(mid, ones_bd) + _dot(lo, ones_bd)


def _params(sem, vmem=VMEM_LIMIT):
    return pltpu.CompilerParams(dimension_semantics=sem, vmem_limit_bytes=vmem)


def _mod_kernel(c_ref, w_ref, b_ref, o_ref):
    c = c_ref[...]
    o_ref[0] = _dot3(_silu(c), w_ref[0]) + b_ref[0]


def _modulation(c_all, w_ada, b_ada):
    rows = c_all.shape[0]
    nt = 3 * D_MODEL // 1024
    return pl.pallas_call(
        _mod_kernel,
        out_shape=jax.ShapeDtypeStruct((DEPTH, rows, 3 * D_MODEL), F32),
        grid=(DEPTH, nt),
        in_specs=[pl.BlockSpec((rows, D_MODEL), lambda l, j: (0, 0)),
                  pl.BlockSpec((1, D_MODEL, 1024), lambda l, j: (l, 0, j)),
                  pl.BlockSpec((1, 1, 1024), lambda l, j: (l, 0, j))],
        out_specs=pl.BlockSpec((1, rows, 1024), lambda l, j: (l, 0, j)),
        compiler_params=_params(("parallel", "parallel")),
        name="adaln_mod",
    )(c_all, w_ada, b_ada.reshape(DEPTH, 1, 3 * D_MODEL))


def _inproj_kernel(x_ref, mod_ref, g_ref, w_ref, o_ref):
    x = x_ref[...]
    ms = jnp.mean(x * x, axis=-1, keepdims=True)
    y = x * lax.rsqrt(ms + NORM_EPS) * g_ref[...]
    mod = mod_ref[0]
    h = y * (1.0 + mod[1:2]) + mod[0:1]
    o_ref[...] = _dot(h.astype(BF16), w_ref[...])


def _inproj(x, mod, norm_g, w_in_bf, tps):
    rows = x.shape[0]
    half = N_IN // 2
    return pl.pallas_call(
        _inproj_kernel,
        out_shape=jax.ShapeDtypeStruct((rows, N_IN), F32),
        grid=(2, rows // ROW_TILE),
        in_specs=[pl.BlockSpec((ROW_TILE, D_MODEL), lambda j, i: (i, 0)),
                  pl.BlockSpec((1, 3, D_MODEL), lambda j, i: (i // tps, 0, 0)),
                  pl.BlockSpec((1, D_MODEL), lambda j, i: (0, 0)),
                  pl.BlockSpec((D_MODEL, half), lambda j, i: (0, j))],
        out_specs=pl.BlockSpec((ROW_TILE, half), lambda j, i: (i, j)),
        compiler_params=_params(("parallel", "parallel")),
        name="inproj",
    )(x, mod, norm_g.reshape(1, D_MODEL), w_in_bf)


def _neighbours(cur, prev_blk, next_blk, first, last):
    rows = cur.shape[0]
    rid = lax.broadcasted_iota(jnp.int32, cur.shape, 0)
    prow = jnp.where(first, 0.0, prev_blk[HALO - 1:HALO, :])
    nrow = jnp.where(last, 0.0, next_blk[0:1, :])
    up = jnp.where(rid == 0, prow, pltpu.roll(cur, 1, 0))
    dn = jnp.where(rid == rows - 1, nrow, pltpu.roll(cur, rows - 1, 0))
    return up, dn


def _halo_specs(width, col, rows):
    per = ROW_TILE // HALO
    nblk = rows // HALO
    cb = col // width
    return [pl.BlockSpec((ROW_TILE, width), lambda i: (i, cb)),
            pl.BlockSpec((HALO, width), lambda i: (jnp.maximum(i * per - 1, 0), cb)),
            pl.BlockSpec((HALO, width), lambda i: (jnp.minimum((i + 1) * per, nblk - 1), cb))]


def _hy_pre_kernel(tps, zc, zp, zn, g_ref, cw, cb, u_ref, x1g_ref):
    i = pl.program_id(0)
    first = (i % tps) == 0
    last = (i % tps) == tps - 1
    z = zc[...]
    up, dn = _neighbours(z, zp[...], zn[...], first, last)
    u = up * cw[0:1, :] + z * cw[1:2, :] + dn * cw[2:3, :] + cb[...]
    v = u[:, :BR_W]
    x1 = u[:, BR_W:2 * BR_W]
    x2 = u[:, 2 * BR_W:]
    u2 = x2 * v
    hi, lo = _split2(u2)
    u_ref[:, :BR_W] = hi
    u_ref[:, BR_W:] = lo
    x1g_ref[...] = x1 * _silu(g_ref[...])


def _hy_pre(z, conv_w, conv_b, tps):
    rows = z.shape[0]
    return pl.pallas_call(
        functools.partial(_hy_pre_kernel, tps),
        out_shape=(jax.ShapeDtypeStruct((rows, 2 * BR_W), BF16),
                   jax.ShapeDtypeStruct((rows, BR_W), F32)),
        grid=(rows // ROW_TILE,),
        in_specs=_halo_specs(3 * BR_W, COL_HY_Z, rows) + [
            pl.BlockSpec((ROW_TILE, BR_W), lambda i: (i, COL_HY_G // BR_W)),
            pl.BlockSpec((3, 3 * BR_W), lambda i: (0, 0)),
            pl.BlockSpec((1, 3 * BR_W), lambda i: (0, 0))],
        out_specs=(pl.BlockSpec((ROW_TILE, 2 * BR_W), lambda i: (i, 0)),
                   pl.BlockSpec((ROW_TILE, BR_W), lambda i: (i, 0))),
        compiler_params=_params(("parallel",)),
        name="hy_pre",
    )(z, z, z, z, conv_w, conv_b.reshape(1, 3 * BR_W))


def _filter_kernel(L, zp, f1, fb1, fr, f2, fb2, f3, o_ref):
    freq = fr[...]
    hid = jnp.sin(freq * (_dot3(zp[...], f1[...]) + fb1[...]))
    hid = jnp.sin(freq * (_dot3(hid, f2[...]) + fb2[...]))
    h = _dot3(hid, f3[...])
    row = lax.broadcasted_iota(jnp.int32, (L, BR_W), 0)
    lane = lax.broadcasted_iota(jnp.int32, (L, BR_W), 1)
    t = row.astype(F32) / float(L - 1)
    delta = HY_MIN_DECAY + lane.astype(F32) * ((HY_MAX_DECAY - HY_MIN_DECAY) / float(BR_W - 1))
    win = jnp.exp(-t * jnp.abs(delta))
    hf = h[:, :BR_W] * win
    hb = jnp.where(row == 0, 0.0, h[:, BR_W:] * win)
    hfh, hfl = _split2(hf)
    hbh, hbl = _split2(hb)
    o_ref[:, 0 * BR_W:1 * BR_W] = hfh
    o_ref[:, 1 * BR_W:2 * BR_W] = hbh
    o_ref[:, 2 * BR_W:3 * BR_W] = hfl
    o_ref[:, 3 * BR_W:4 * BR_W] = hbl


def _hy_filters(L, zpos, f1p, fb1, freq, f2, fb2, f3):
    return pl.pallas_call(
        functools.partial(_filter_kernel, L),
        out_shape=jax.ShapeDtypeStruct((L, 4 * BR_W), BF16),
        compiler_params=_params(None),
        name="hy_filter",
    )(zpos, f1p, fb1.reshape(1, HY_FFN), freq.reshape(1, HY_FFN), f2, fb2.reshape(1, HY_FFN), f3)


def _filter_dft_kernel(tk, L, ch, cl, sh, sl, hfb, hre_ref, him_ref):
    j = pl.program_id(0)
    xh = hfb[:, :2 * BR_W]
    xl = hfb[:, 2 * BR_W:]
    fre = _dot3s(ch[...], cl[...], xh, xl)
    fim = _dot3s(sh[...], sl[...], xh, xl)
    row = lax.broadcasted_iota(jnp.int32, (tk, BR_W), 0) + j * tk
    is0 = row == 0
    scale = jnp.where(is0, 1.0 / (2 * L), 2.0 / (2 * L))
    hre_ref[...] = (fre[:, :BR_W] + fre[:, BR_W:]) * scale
    him_ref[...] = jnp.where(is0, fim[:, :BR_W] + fim[:, BR_W:], fim[:, :BR_W] - fim[:, BR_W:]) * scale


def _filter_dft(L, tk, tabs, hfb):
    ch, cl, sh, sl = tabs[:4]
    tab = pl.BlockSpec((tk, L), lambda j: (j, 0))
    return pl.pallas_call(
        functools.partial(_filter_dft_kernel, tk, L),
        out_shape=(jax.ShapeDtypeStruct((L, BR_W), F32), jax.ShapeDtypeStruct((L, BR_W), F32)),
        grid=(L // tk,),
        in_specs=[tab, tab, tab, tab, pl.BlockSpec((L, 4 * BR_W), lambda j: (0, 0))],
        out_specs=(pl.BlockSpec((tk, BR_W), lambda j: (j, 0)), pl.BlockSpec((tk, BR_W), lambda j: (j, 0))),
        compiler_params=_params(("parallel",)),
        name="hy_filter_dft",
    )(ch, cl, sh, sl, hfb)


def _dft_fwd_kernel(tk, ch, cl, sh, sl, u_ref, hre_ref, him_ref, y_ref):
    j = pl.program_id(1)
    uh = u_ref[:, :BR_W]
    ul = u_ref[:, BR_W:]
    ure = _dot3s(ch[...], cl[...], uh, ul)
    uim = _dot3s(sh[...], sl[...], uh, ul)
    hre = hre_ref[...]
    him = him_ref[...]
    row = lax.broadcasted_iota(jnp.int32, (tk, BR_W), 0) + j * tk
    is0 = row == 0
    yre = jnp.where(is0, ure * hre, ure * hre - uim * him)
    yim = jnp.where(is0, uim * him, ure * him + uim * hre)
    a, b = _split2(yre)
    c, d = _split2(yim)
    y_ref[:, 0 * BR_W:1 * BR_W] = a
    y_ref[:, 1 * BR_W:2 * BR_W] = b
    y_ref[:, 2 * BR_W:3 * BR_W] = c
    y_ref[:, 3 * BR_W:4 * BR_W] = d


def _dft_fwd(B, L, tk, tabs, u2, hre, him):
    ch, cl, sh, sl = tabs[:4]
    nk = L // tk
    tab = pl.BlockSpec((tk, L), lambda b, j: (j, 0))
    hsp = pl.BlockSpec((tk, BR_W), lambda b, j: (j, 0))
    return pl.pallas_call(
        functools.partial(_dft_fwd_kernel, tk),
        out_shape=jax.ShapeDtypeStruct((B * L, 4 * BR_W), BF16),
        grid=(B, nk),
        in_specs=[tab, tab, tab, tab, pl.BlockSpec((L, 2 * BR_W), lambda b, j: (b, 0)), hsp, hsp],
        out_specs=pl.BlockSpec((tk, 4 * BR_W), lambda b, j: (b * nk + j, 0)),
        compiler_params=_params(("parallel", "parallel")),
        name="hy_dft_fwd",
    )(ch, cl, sh, sl, u2, hre, him)


def _dft_inv_kernel(ch, cl, sth, stl, y_ref, u_ref, x1g_ref, bias_ref, o_ref):
    conv = (_dot3s(ch[...], cl[...], y_ref[:, 0 * BR_W:1 * BR_W], y_ref[:, 1 * BR_W:2 * BR_W])
            + _dot3s(sth[...], stl[...], y_ref[:, 2 * BR_W:3 * BR_W], y_ref[:, 3 * BR_W:4 * BR_W]))
    u2 = u_ref[:, :BR_W].astype(F32) + u_ref[:, BR_W:].astype(F32)
    o_ref[...] = (conv + u2 * bias_ref[...]) * x1g_ref[...]


def _dft_inv(B, L, tm, tabs, y4, u2, x1g, bias):
    ch, cl, _, _, sth, stl = tabs
    nt = L // tm
    tab = pl.BlockSpec((tm, L), lambda b, i: (i, 0))
    return pl.pallas_call(
        _dft_inv_kernel,
        out_shape=jax.ShapeDtypeStruct((B * L, BR_W), F32),
        grid=(B, nt),
        in_specs=[tab, tab, tab, tab,
                  pl.BlockSpec((L, 4 * BR_W), lambda b, i: (b, 0)),
                  pl.BlockSpec((tm, 2 * BR_W), lambda b, i: (b * nt + i, 0)),
                  pl.BlockSpec((tm, BR_W), lambda b, i: (b * nt + i, 0)),
                  pl.BlockSpec((1, BR_W), lambda b, i: (0, 0))],
        out_specs=pl.BlockSpec((tm, BR_W), lambda b, i: (b * nt + i, 0)),
        compiler_params=_params(("parallel", "parallel")),
        name="hy_dft_inv",
    )(ch, cl, sth, stl, y4, u2, x1g, bias.reshape(1, BR_W))


def _dft_tables(L):
    n = 2 * L
    k = jnp.arange(L, dtype=jnp.int32)
    m = (k[:, None] * k[None, :]) % n
    ang = m.astype(F32) * (2.0 * math.pi / n)
    c = jnp.cos(ang)
    s = -jnp.sin(ang)
    alt = jnp.where(k % 2 == 0, 1.0, -1.0).astype(F32)
    s = jnp.where((k == 0)[:, None], alt[None, :], s)
    out = []
    for t in (c, s, s.T):
        hi = t.astype(BF16)
        lo = (t - hi.astype(F32)).astype(BF16)
        out += [hi, lo]
    return tuple(out)


def _hy_positions(L):
    t = jnp.linspace(0.0, 1.0, L, dtype=F32)[:, None]
    w = 2.0 * math.pi * jnp.arange(L, dtype=F32)[:, None] / L
    bands = jnp.linspace(1e-4, HY_BANDS - 1, HY_BANDS, dtype=F32)[None, :]
    z = jnp.concatenate([t, jnp.cos(bands * w), -jnp.sin(bands * w)], axis=-1)
    return jnp.pad(z, ((0, 0), (0, LANES - HY_EMB)))


def _rw_prep_kernel(tps, rc, rp, rn, kc, kp, kn, vc, vp, vn, lc, lp, ln,
                    mu_r, mu_k, mu_v, mu_l, kkw, ka, rk, w0, a0, w2, a2, ones_bd,
                    r_out, nkk_out, v_out, bonus_out, wf, bf, kdf, wb, bb, kdb):
    i = pl.program_id(0)
    first = (i % tps) == 0
    last = (i % tps) == tps - 1

    def shifted(cur_ref, prev_ref, next_ref, mu_ref):
        cur = cur_ref[...]
        up, dn = _neighbours(cur, prev_ref[...], next_ref[...], first, last)
        return cur + mu_ref[...] * (0.5 * (up + dn) - cur)

    r = shifted(rc, rp, rn, mu_r)
    k = shifted(kc, kp, kn, mu_k)
    v = shifted(vc, vp, vn, mu_v)
    lora = shifted(lc, lp, ln, mu_l)
    ones = ones_bd[...]

    kk = k * kkw[...]
    kk = kk * lax.rsqrt(_segsum(kk * kk, ones) + 1e-12)
    r_out[...] = r
    v_out[...] = v
    nkk_out[...] = -kk

    kd_sum = None
    outs = ((wf, bf, kdf), (wb, bb, kdb))
    for d in range(2):
        wl = lora[:, d * W_LORA:(d + 1) * W_LORA]
        al = lora[:, 2 * W_LORA + d * A_LORA:2 * W_LORA + (d + 1) * A_LORA]
        w = jnp.exp(-RW_DECAY_SCALE * _sigmoid(w0[d:d + 1, :] + _dot3(jnp.tanh(wl), w2[d])))
        a = _sigmoid(a0[d:d + 1, :] + _dot3(al, a2[d]))
        kd = k * (1.0 + (a - 1.0) * ka[...])
        outs[d][0][...] = w
        outs[d][1][...] = kk * a
        outs[d][2][...] = kd
        kd_sum = kd if kd_sum is None else kd_sum + kd
    bonus_out[...] = _segsum(r * (0.5 * kd_sum) * rk[...], ones) * v


def _rw_prep(z, p, ones_bd, tps):
    rows = z.shape[0]
    mu = p["rw_mu"]
    vec = lambda a: a.reshape(1, -1)
    full = lambda a: pl.BlockSpec(a.shape, lambda i: (0,) * a.ndim)
    smalls = [vec(mu[0:512]), vec(mu[512:1024]), vec(mu[1024:1536]), vec(mu[1536:1792]),
              vec(p["rw_kk"]), vec(p["rw_ka"]), vec(p["rw_rk"]), p["rw_w0"], p["rw_a0"],
              p["rw_w2"], p["rw_a2"], ones_bd]
    tile = pl.BlockSpec((ROW_TILE, BR_W), lambda i: (i, 0))
    return pl.pallas_call(
        functools.partial(_rw_prep_kernel, tps),
        out_shape=tuple(jax.ShapeDtypeStruct((rows, BR_W), F32) for _ in range(10)),
        grid=(rows // ROW_TILE,),
        in_specs=(_halo_specs(BR_W, COL_RW_R, rows) + _halo_specs(BR_W, COL_RW_K, rows)
                  + _halo_specs(BR_W, COL_RW_V, rows) + _halo_specs(2 * LANES, COL_LORA, rows)
                  + [full(a) for a in smalls]),
        out_specs=tuple(tile for _ in range(10)),
        compiler_params=_params(("parallel",)),
        name="rw_prep",
    )(*([z] * 12), *smalls)


def _scan_kernel(nc, r_f, nk_f, w_f, b_f, kd_f, vt_f, r_b, nk_b, w_b, b_b, kd_b, vt_b,
                 s0, ones2, onehot, yt_f, yt_b, sfin, state, vth, vtl):
    c = pl.program_id(1)

    @pl.when(c == 0)
    def _():
        state[...] = s0[0]

    g2 = ones2[...]
    lane = lax.broadcasted_iota(jnp.int32, (RW_N, LANES), 1) & (SUB - 1)
    dirs = ((r_f, nk_f, w_f, b_f, kd_f, vt_f, yt_f), (r_b, nk_b, w_b, b_b, kd_b, vt_b, yt_b))
    npair = RW_H // 2

    for sub in range(SCAN_CHUNK // SUB):
        subs = (sub, SCAN_CHUNK // SUB - 1 - sub)
        for d in range(2):
            for p in range(npair):
                hi, lo = _split2(dirs[d][5][0, subs[d], p])
                vth[d, p] = hi
                vtl[d, p] = lo
                dirs[d][6][0, subs[d], p] = jnp.zeros((RW_N, LANES), F32)

        def body(g, carry):
            for d in range(2):
                r_ref, nk_ref, w_ref, b_ref, kd_ref, _, y_ref = dirs[d]
                gg = g if d == 0 else SUB // HALO - 1 - g
                base = pl.multiple_of(gg * HALO, HALO)
                rows = pl.ds(subs[d] * SUB + base, HALO)
                oh = onehot[pl.ds(base, HALO), :]
                for p in range(npair):
                    cs = pl.ds(p * LANES, LANES)
                    nk8, w8, b8 = nk_ref[rows, cs], w_ref[rows, cs], b_ref[rows, cs]
                    kd8, r8 = kd_ref[rows, cs], r_ref[rows, cs]
                    st = state[d, p]
                    yt = y_ref[0, subs[d], p]
                    for jj in range(HALO):
                        j = jj if d == 0 else HALO - 1 - jj
                        row = lambda a: a[j:j + 1, :]
                        sel = lane == base + j
                        mb = jnp.broadcast_to(row(oh), (RW_N, LANES)).astype(BF16)
                        zh, zl = _split2(st * row(nk8))
                        sa = _dot(jnp.concatenate([zh, zl], axis=1), g2)
                        vb = _dot(jnp.concatenate([vth[d, p] * mb, vtl[d, p] * mb], axis=1), g2)
                        st = st * row(w8) + sa * row(b8) + vb * row(kd8)
                        yh, yl = _split2(st * row(r8))
                        yt = jnp.where(sel, _dot(jnp.concatenate([yh, yl], axis=1), g2), yt)
                    state[d, p] = st
                    y_ref[0, subs[d], p] = yt
            return carry

        lax.fori_loop(0, SUB // HALO, body, 0)

    @pl.when(c == nc - 1)
    def _():
        sfin[0] = state[...]


def _rw_scan(B, L, seqs, vt, s0, ones2, onehot):
    r, nkk, wf, bf, kdf, wb, bb, kdb = seqs
    nc = L // SCAN_CHUNK
    nsub = SCAN_CHUNK // SUB
    npair = RW_H // 2
    fwd = pl.BlockSpec((SCAN_CHUNK, BR_W), lambda b, c: (b * nc + c, 0))
    bwd = pl.BlockSpec((SCAN_CHUNK, BR_W), lambda b, c: (b * nc + nc - 1 - c, 0))
    tile = (1, nsub, npair, RW_N, LANES)
    vfwd = pl.BlockSpec(tile, lambda b, c: (b, c, 0, 0, 0))
    vbwd = pl.BlockSpec(tile, lambda b, c: (b, nc - 1 - c, 0, 0, 0))
    sspec = pl.BlockSpec((1, 2, npair, RW_N, LANES), lambda b, c: (b, 0, 0, 0, 0))
    yshape = jax.ShapeDtypeStruct((B, L // SUB, npair, RW_N, LANES), F32)
    return pl.pallas_call(
        functools.partial(_scan_kernel, nc),
        out_shape=(yshape, yshape, jax.ShapeDtypeStruct((B, 2, npair, RW_N, LANES), F32)),
        grid=(B, nc),
        in_specs=[fwd, fwd, fwd, fwd, fwd, vfwd, bwd, bwd, bwd, bwd, bwd, vbwd, sspec,
                  pl.BlockSpec((2 * LANES, LANES), lambda b, c: (0, 0)),
                  pl.BlockSpec((SUB, LANES), lambda b, c: (0, 0))],
        out_specs=(vfwd, vbwd, sspec),
        scratch_shapes=[pltpu.VMEM((2, npair, RW_N, LANES), F32),
                        pltpu.VMEM((2, npair, RW_N, LANES), BF16),
                        pltpu.VMEM((2, npair, RW_N, LANES), BF16)],
        compiler_params=_params(("parallel", "arbitrary")),
        name="rw_scan",
    )(r, nkk, wf, bf, kdf, vt, r, nkk, wb, bb, kdb, vt, s0, ones2, onehot)


def _rw_post_kernel(yf, yb, bonus, g_ref, lnw, lnb, ones_bd, o_ref):
    ones = ones_bd[...]
    y = yf[...] + yb[...]
    mu = _segsum(y, ones) * (1.0 / RW_N)
    yc = y - mu
    var = _segsum(yc * yc, ones) * (1.0 / RW_N)
    y = yc * lax.rsqrt(var + RW_GN_EPS) * lnw[...] + lnb[...]
    o_ref[...] = (y + bonus[...]) * _silu(g_ref[...])


def _rw_post(yf, yb, bonus, z, ln_w, ln_b, ones_bd):
    rows = yf.shape[0]
    tile = pl.BlockSpec((ROW_TILE, BR_W), lambda i: (i, 0))
    vec = pl.BlockSpec((1, BR_W), lambda i: (0, 0))
    return pl.pallas_call(
        _rw_post_kernel,
        out_shape=jax.ShapeDtypeStruct((rows, BR_W), F32),
        grid=(rows // ROW_TILE,),
        in_specs=[tile, tile, tile, pl.BlockSpec((ROW_TILE, BR_W), lambda i: (i, COL_RW_G // BR_W)),
                  vec, vec, pl.BlockSpec((BR_W, BR_W), lambda i: (0, 0))],
        out_specs=tile,
        compiler_params=_params(("parallel",)),
        name="rw_post",
    )(yf, yb, bonus, z, ln_w.reshape(1, BR_W), ln_b.reshape(1, BR_W), ones_bd)


def _rope(x, cos, sin_signed):
    lane = lax.broadcasted_iota(jnp.int32, x.shape, 1)
    low = (lane & (ROPE_AXIS - 1)) < (ROPE_AXIS // 2)
    width = x.shape[1]
    partner = jnp.where(low, pltpu.roll(x, width - ROPE_AXIS // 2, 1), pltpu.roll(x, ROPE_AXIS // 2, 1))
    return x * cos + partner * sin_signed


def _da_prep_kernel(rope, q_ref, k_ref, gq, gk, ones_bd, *rest):
    if rope:
        cos_ref, sin_ref, q_out, k_out, kn_out = rest
    else:
        q_out, k_out, kn_out = rest
    ones = ones_bd[...]

    def norm(x, g):
        ms = _segsum(x * x, ones) * (1.0 / DA_DH)
        return x * lax.rsqrt(ms + NORM_EPS) * g

    qn = norm(q_ref[...], gq[...])
    kn = norm(k_ref[...], gk[...])
    kn_out[...] = kn
    if rope:
        qn = _rope(qn, cos_ref[...], sin_ref[...])
        kn = _rope(kn, cos_ref[...], sin_ref[...])
    k_out[...] = kn.astype(BF16)
    qn = qn * (DA_DH ** -0.5)
    lane = lax.broadcasted_iota(jnp.int32, (qn.shape[0], LANES), 1)
    for h in range(DA_H):
        qh = qn[:, h * LANES:(h + 1) * LANES]
        q_out[:, (2 * h) * LANES:(2 * h + 1) * LANES] = jnp.where(lane < DA_DH, qh, 0.0).astype(BF16)
        q_out[:, (2 * h + 1) * LANES:(2 * h + 2) * LANES] = jnp.where(lane >= DA_DH, qh, 0.0).astype(BF16)


def _da_prep(z, gq, gk, ones_bd, tps, rope_tabs):
    rows = z.shape[0]
    rope = rope_tabs is not None
    tile = lambda col: pl.BlockSpec((ROW_TILE, BR_W), lambda i: (i, col // BR_W))
    vec = pl.BlockSpec((1, BR_W), lambda i: (0, 0))
    in_specs = [tile(COL_DA_Q), tile(COL_DA_K), vec, vec, pl.BlockSpec((BR_W, BR_W), lambda i: (0, 0))]
    args = [z, z, jnp.tile(gq, 2 * DA_H).reshape(1, BR_W), jnp.tile(gk, 2 * DA_H).reshape(1, BR_W), ones_bd]
    if rope:
        tab = pl.BlockSpec((ROW_TILE, BR_W), lambda i: (i % tps, 0))
        in_specs += [tab, tab]
        args += list(rope_tabs)
    return pl.pallas_call(
        functools.partial(_da_prep_kernel, rope),
        out_shape=(jax.ShapeDtypeStruct((rows, 2 * BR_W), BF16),
                   jax.ShapeDtypeStruct((rows, BR_W), BF16),
                   jax.ShapeDtypeStruct((rows, BR_W), F32)),
        grid=(rows // ROW_TILE,),
        in_specs=in_specs,
        out_specs=(pl.BlockSpec((ROW_TILE, 2 * BR_W), lambda i: (i, 0)),
                   pl.BlockSpec((ROW_TILE, BR_W), lambda i: (i, 0)),
                   pl.BlockSpec((ROW_TILE, BR_W), lambda i: (i, 0))),
        compiler_params=_params(("parallel",)),
        name="da_prep",
    )(*args)


def _rope_tables(L):
    half = ROPE_AXIS // 2
    lane = jnp.arange(BR_W)
    j = lane % DA_DH
    use_col = (j // ROPE_AXIS) == 1
    idx = (j % ROPE_AXIS) % half
    freqs = ROPE_BASE ** (-idx.astype(F32) / half)
    t = jnp.arange(L)
    pos = jnp.where(use_col[None, :], (t % GRID_W)[:, None], (t // GRID_W)[:, None]).astype(F32)
    ang = pos * freqs[None, :]
    sign = jnp.where((j % ROPE_AXIS) < half, -1.0, 1.0).astype(F32)
    return jnp.cos(ang), jnp.sin(ang) * sign[None, :]


def _da_attn_kernel(lam_init, q_ref, k_ref, v_ref, g_ref, gsub, lq1, lk1, lq2, lk2, o_ref):
    lam = (jnp.exp(jnp.sum(lq1[...] * lk1[...], axis=-1, keepdims=True))
           - jnp.exp(jnp.sum(lq2[...] * lk2[...], axis=-1, keepdims=True)) + lam_init)
    nt = (((1,), (1,)), ((), ()))
    for h in range(DA_H):
        kh = k_ref[0, :, h * LANES:(h + 1) * LANES]
        vh = v_ref[0, :, h * LANES:(h + 1) * LANES]
        outs = []
        for m in range(2):
            q = q_ref[:, (2 * h + m) * LANES:(2 * h + m + 1) * LANES]
            s = lax.dot_general(q, kh, nt, preferred_element_type=F32)
            e = jnp.exp(s - jnp.max(s, axis=-1, keepdims=True))
            den = jnp.sum(e, axis=-1, keepdims=True)
            outs.append(_dot(e.astype(BF16), vh) / den)
        o = outs[0] - lam * outs[1]
        o = o * lax.rsqrt(jnp.mean(o * o, axis=-1, keepdims=True) + 1e-5) * gsub[...] * (1.0 - lam_init)
        o_ref[:, h * LANES:(h + 1) * LANES] = o * _silu(g_ref[:, h * LANES:(h + 1) * LANES])


def _da_attn(B, L, lam_init, q2, kcat, vcat, z, p):
    lk = kcat.shape[1]
    nt = L // ROW_TILE
    vec = lambda a: a.reshape(1, -1)
    small = lambda n: pl.BlockSpec((1, n), lambda b, i: (0, 0))
    return pl.pallas_call(
        functools.partial(_da_attn_kernel, lam_init),
        out_shape=jax.ShapeDtypeStruct((B * L, BR_W), F32),
        grid=(B, nt),
        in_specs=[pl.BlockSpec((ROW_TILE, 2 * BR_W), lambda b, i: (b * nt + i, 0)),
                  pl.BlockSpec((1, lk, BR_W), lambda b, i: (b, 0, 0)),
                  pl.BlockSpec((1, lk, BR_W), lambda b, i: (b, 0, 0)),
                  pl.BlockSpec((ROW_TILE, BR_W), lambda b, i: (b * nt + i, COL_DA_G // BR_W)),
                  small(2 * DA_DH), small(DA_DH), small(DA_DH), small(DA_DH), small(DA_DH)],
        out_specs=pl.BlockSpec((ROW_TILE, BR_W), lambda b, i: (b * nt + i, 0)),
        compiler_params=_params(("parallel", "parallel")),
        name="da_attn",
    )(q2, kcat, vcat, z, vec(p["da_gsub"]), vec(p["da_lq1"]), vec(p["da_lk1"]),
      vec(p["da_lq2"]), vec(p["da_lk2"]))


def _merge_kernel(ya, yb, yc, mga, mgb, mgc, x_ref, mod_ref, wbr, wout, o_ref):
    merged = (_sigmoid(mga[...]) * _dot(ya[...].astype(BF16), wbr[0])
              + _sigmoid(mgb[...]) * _dot(yb[...].astype(BF16), wbr[1])
              + _sigmoid(mgc[...]) * _dot(yc[...].astype(BF16), wbr[2]))
    out = _dot(merged.astype(BF16), wout[...])
    o_ref[...] = x_ref[...] + mod_ref[0][2:3] * out


def _merge(ya, yb, yc, z, x, mod, wbr_bf, wout_bf, tps):
    rows = x.shape[0]
    tile = pl.BlockSpec((ROW_TILE, BR_W), lambda i: (i, 0))
    mg = lambda n: pl.BlockSpec((ROW_TILE, D_MODEL), lambda i: (i, COL_MG // D_MODEL + n))
    return pl.pallas_call(
        _merge_kernel,
        out_shape=jax.ShapeDtypeStruct((rows, D_MODEL), F32),
        grid=(rows // ROW_TILE,),
        in_specs=[tile, tile, tile, mg(0), mg(1), mg(2),
                  pl.BlockSpec((ROW_TILE, D_MODEL), lambda i: (i, 0)),
                  pl.BlockSpec((1, 3, D_MODEL), lambda i: (i // tps, 0, 0)),
                  pl.BlockSpec((3, BR_W, D_MODEL), lambda i: (0, 0, 0)),
                  pl.BlockSpec((D_MODEL, D_MODEL), lambda i: (0, 0))],
        out_specs=pl.BlockSpec((ROW_TILE, D_MODEL), lambda i: (i, 0)),
        compiler_params=_params(("parallel",)),
        name="merge_out",
    )(ya, yb, yc, z, z, z, x, mod, wbr_bf, wout_bf)


def _to_value_tiles(v, B, L):
    t = v.reshape(B, L // SUB, SUB, RW_H // 2, 2, RW_N)
    return t.transpose(0, 1, 3, 5, 4, 2).reshape(B, L // SUB, RW_H // 2, RW_N, 2 * SUB)


def _from_value_tiles(yt, B, L):
    t = yt.reshape(B, L // SUB, RW_H // 2, RW_N, 2, SUB)
    return t.transpose(0, 1, 5, 2, 4, 3).reshape(B * L, BR_W)


def _state_to_pairs(s):
    B = s.shape[0]
    t = s.reshape(B, 2, RW_H // 2, 2, RW_N, RW_N)
    return t.transpose(0, 1, 2, 4, 3, 5).reshape(B, 2, RW_H // 2, RW_N, 2 * RW_N)


def _state_from_pairs(s):
    B = s.shape[0]
    t = s.reshape(B, 2, RW_H // 2, RW_N, 2, RW_N)
    return t.transpose(0, 1, 2, 4, 3, 5).reshape(B, 2, RW_H, RW_N, RW_N)


def _permute_w_in(w):
    hy_z, hy_g = w[:, 0:1536], w[:, 1536:2048]
    rw_rkv, rw_lora, rw_g = w[:, 2048:3584], w[:, 3584:3840], w[:, 3840:4352]
    da = w[:, 4352:6400]
    mg = w[:, 6400:9472]
    return jnp.concatenate([hy_z, hy_g, rw_rkv, rw_g, da, mg, rw_lora], axis=1)


def _layer(x, mod, p, consts, B, L, lam_init, ctx):
    tps = L // ROW_TILE
    tabs, zpos, rope_tabs, ones_bd, ones2, onehot = consts
    z = _inproj(x, mod, p["norm_g"], p["w_in_bf"], tps)

    u2, x1g = _hy_pre(z, p["hy_conv_w"], p["hy_conv_b"], tps)
    tk = min(L, 256)
    hfb = _hy_filters(L, zpos, p["hy_f1p"], p["hy_fb1"], p["hy_freq"], p["hy_f2"], p["hy_fb2"], p["hy_f3"])
    hre, him = _filter_dft(L, tk, tabs, hfb)
    y4 = _dft_fwd(B, L, tk, tabs, u2, hre, him)
    y_a = _dft_inv(B, L, min(L, 512), tabs, y4, u2, x1g, p["hy_bias"])

    r, nkk, v, bonus, wf, bf, kdf, wb, bb, kdb = _rw_prep(z, p, ones_bd, tps)
    npair = RW_H // 2
    if ctx is None:
        s0 = jnp.zeros((B, 2, npair, RW_N, LANES), F32)
    else:
        s0 = _state_to_pairs(ctx[2])
    ytf, ytb, sfin = _rw_scan(B, L, (r, nkk, wf, bf, kdf, wb, bb, kdb), _to_value_tiles(v, B, L),
                              s0, ones2, onehot)
    y_b = _rw_post(_from_value_tiles(ytf, B, L), _from_value_tiles(ytb, B, L), bonus, z,
                   p["rw_ln_w"], p["rw_ln_b"], ones_bd)

    q2, k_bf, kn = _da_prep(z, p["da_gq"], p["da_gk"], ones_bd, tps, rope_tabs if ctx is not None else None)
    v_da = z[:, COL_DA_V:COL_DA_V + BR_W]
    k3 = k_bf.reshape(B, L, BR_W)
    v3 = v_da.astype(BF16).reshape(B, L, BR_W)
    if ctx is not None:
        k3 = jnp.concatenate([ctx[0].reshape(B, -1, BR_W).astype(BF16), k3], axis=1)
        v3 = jnp.concatenate([ctx[1].reshape(B, -1, BR_W).astype(BF16), v3], axis=1)
    y_c = _da_attn(B, L, lam_init, q2, k3, v3, z, p)

    x_new = _merge(y_a, y_b, y_c, z, x, mod, p["w_br_bf"], p["w_out_bf"], tps)
    return x_new, (kn, v_da, sfin)


def kernel(x_prompt, x_sample, cache_k, cache_v, state_rwkv, c, c_ctx, norm_g, w_ada, b_ada, w_in, hy_conv_w, hy_conv_b, hy_f1, hy_fb1, hy_freq, hy_f2, hy_fb2, hy_f3, hy_bias, rw_mu, rw_w0, rw_w2, rw_a0, rw_a2, rw_kk, rw_ka, rw_rk, rw_ln_w, rw_ln_b, da_gq, da_gk, da_lq1, da_lk1, da_lq2, da_lk2, da_gsub, w_br, w_out):
    Bp, Lp, _ = x_prompt.shape
    Bs, Ls, _ = x_sample.shape

    pad = (-(Bp + Bs)) % 8
    c_all = jnp.concatenate([jnp.broadcast_to(c_ctx, (Bp, D_MODEL)), c, jnp.zeros((pad, D_MODEL), F32)], axis=0)
    mod_all = _modulation(c_all, w_ada, b_ada)
    mod_p = mod_all[:, :Bp].reshape(DEPTH, Bp, 3, D_MODEL)
    mod_s = mod_all[:, Bp:Bp + Bs].reshape(DEPTH, Bs, 3, D_MODEL)

    seg = jnp.arange(BR_W) // RW_N
    ones_bd = (seg[:, None] == seg[None, :]).astype(BF16)
    pair_seg = jnp.arange(LANES) // RW_N
    g = (pair_seg[:, None] == pair_seg[None, :]).astype(BF16)
    ones2 = jnp.concatenate([g, g], axis=0)
    step = jnp.arange(SUB)
    onehot = (step[:, None] == (jnp.arange(LANES) % SUB)[None, :]).astype(F32)

    consts = {}
    for L in sorted({Lp, Ls}):
        consts[L] = (_dft_tables(L), _hy_positions(L), _rope_tables(L), ones_bd, ones2, onehot)

    def layer_params(l):
        return dict(norm_g=norm_g[l], w_in_bf=_permute_w_in(w_in[l]).astype(BF16),
                    hy_conv_w=hy_conv_w[l], hy_conv_b=hy_conv_b[l],
                    hy_f1p=jnp.pad(hy_f1[l], ((0, LANES - HY_EMB), (0, 0))), hy_fb1=hy_fb1[l],
                    hy_freq=hy_freq[l], hy_f2=hy_f2[l], hy_fb2=hy_fb2[l], hy_f3=hy_f3[l],
                    hy_bias=hy_bias[l], rw_mu=rw_mu[l], rw_w0=rw_w0[l], rw_w2=rw_w2[l],
                    rw_a0=rw_a0[l], rw_a2=rw_a2[l], rw_kk=rw_kk[l], rw_ka=rw_ka[l], rw_rk=rw_rk[l],
                    rw_ln_w=rw_ln_w[l], rw_ln_b=rw_ln_b[l], da_gq=da_gq[l], da_gk=da_gk[l],
                    da_lq1=da_lq1[l], da_lk1=da_lk1[l], da_lq2=da_lq2[l], da_lk2=da_lk2[l],
                    da_gsub=da_gsub[l], w_br_bf=w_br[l].astype(BF16), w_out_bf=w_out[l].astype(BF16))

    params = [layer_params(l) for l in range(DEPTH)]

    xp = x_prompt.reshape(Bp * Lp, D_MODEL)
    new_k, new_v, new_s = [], [], []
    for l in range(DEPTH):
        lam_init = 0.8 - 0.6 * math.exp(-0.3 * l)
        xp, (k_l, v_l, s_l) = _layer(xp, mod_p[l], params[l], consts[Lp], Bp, Lp, lam_init, None)
        new_k.append(k_l.reshape(Bp, Lp, DA_H, 2, DA_DH))
        new_v.append(v_l.reshape(Bp, Lp, DA_H, 2 * DA_DH))
        new_s.append(_state_from_pairs(s_l))
    new_cache_k = jnp.stack(new_k, axis=1)
    new_cache_v = jnp.stack(new_v, axis=1)
    new_state = jnp.stack(new_s, axis=1)

    xs = x_sample.reshape(Bs * Ls, D_MODEL)
    for l in range(DEPTH):
        lam_init = 0.8 - 0.6 * math.exp(-0.3 * l)
        ctx = (cache_k[:, l], cache_v[:, l], state_rwkv[:, l])
        xs, _ = _layer(xs, mod_s[l], params[l], consts[Ls], Bs, Ls, lam_init, ctx)

    return (xp.reshape(Bp, Lp, D_MODEL), xs.reshape(Bs, Ls, D_MODEL), new_cache_k, new_cache_v, new_state)
```

```python
import functools
import math

import jax
import jax.numpy as jnp
from jax import lax
from jax.experimental import pallas as pl
from jax.experimental.pallas import tpu as pltpu

F32 = jnp.float32
BF16 = jnp.bfloat16

D_MODEL = 1024
DEPTH = 4
GRID_W = 64
BR_W = 512
HY_EMB = 33
HY_BANDS = (HY_EMB - 1) // 2
HY_FFN = 64
HY_MAX_DECAY = math.log(1e-2) / 0.3
HY_MIN_DECAY = math.log(1e-2) / 1.5
RW_N = 64
RW_H = BR_W // RW_N
W_LORA = 64
A_LORA = 64
RW_DECAY_SCALE = math.exp(-0.5)
RW_GN_EPS = 64e-5
DA_DH = 64
DA_H = BR_W // (2 * DA_DH)
ROPE_AXIS = DA_DH // 2
ROPE_BASE = 10000.0
NORM_EPS = 1e-6
N_IN = 9472

LANES = 128
ROW_TILE = 256
HALO = 8
SCAN_CHUNK = 128
SCAN_BATCH = 2
SUB = 64
VMEM_LIMIT = 52 * 1024 * 1024

COL_HY_Z = 0
COL_HY_G = 1536
COL_RW_R = 2048
COL_RW_K = 2560
COL_RW_V = 3072
COL_RW_G = 3584
COL_DA_Q = 4096
COL_DA_K = 4608
COL_DA_V = 5120
COL_DA_G = 5632
COL_MG = 6144
COL_LORA = 9216


def _sigmoid(x):
    return 1.0 / (1.0 + jnp.exp(-x))


def _silu(x):
    return x * _sigmoid(x)


def _dot(a, b):
    return jnp.dot(a, b, preferred_element_type=F32)


def _split2(x):
    hi = x.astype(BF16)
    lo = (x - hi.astype(F32)).astype(BF16)
    return hi, lo


def _split3(x):
    hi = x.astype(BF16)
    r1 = x - hi.astype(F32)
    mid = r1.astype(BF16)
    lo = (r1 - mid.astype(F32)).astype(BF16)
    return hi, mid, lo


def _dot3(a, b):
    ah, al = _split2(a)
    bh, bl = _split2(b)
    return _dot(ah, bh) + _dot(ah, bl) + _dot(al, bh)


def _dot3s(ah, al, bh, bl):
    return _dot(ah, bh) + _dot(ah, bl) + _dot(al, bh)


def _segsum(x, ones_bd):
    hi, mid, lo = _split3(x)
    return _dot(hi, ones_bd) + _dot(mid, ones_bd) + _dot(lo, ones_bd)


def _params(sem, vmem=VMEM_LIMIT):
    return pltpu.CompilerParams(dimension_semantics=sem, vmem_limit_bytes=vmem)


def _mod_kernel(c_ref, w_ref, b_ref, o_ref):
    c = c_ref[...]
    o_ref[0] = _dot3(_silu(c), w_ref[0]) + b_ref[0]


def _modulation(c_all, w_ada, b_ada):
    rows = c_all.shape[0]
    nt = 3 * D_MODEL // 1024
    return pl.pallas_call(
        _mod_kernel,
        out_shape=jax.ShapeDtypeStruct((DEPTH, rows, 3 * D_MODEL), F32),
        grid=(DEPTH, nt),
        in_specs=[pl.BlockSpec((rows, D_MODEL), lambda l, j: (0, 0)),
                  pl.BlockSpec((1, D_MODEL, 1024), lambda l, j: (l, 0, j)),
                  pl.BlockSpec((1, 1, 1024), lambda l, j: (l, 0, j))],
        out_specs=pl.BlockSpec((1, rows, 1024), lambda l, j: (l, 0, j)),
        compiler_params=_params(("parallel", "parallel")),
        name="adaln_mod",
    )(c_all, w_ada, b_ada.reshape(DEPTH, 1, 3 * D_MODEL))


def _inproj_kernel(x_ref, mod_ref, g_ref, w_ref, o_ref):
    x = x_ref[...]
    ms = jnp.mean(x * x, axis=-1, keepdims=True)
    y = x * lax.rsqrt(ms + NORM_EPS) * g_ref[...]
    mod = mod_ref[0]
    h = y * (1.0 + mod[1:2]) + mod[0:1]
    o_ref[...] = _dot(h.astype(BF16), w_ref[...])


def _inproj(x, mod, norm_g, w_in_bf, tps):
    rows = x.shape[0]
    half = N_IN // 2
    return pl.pallas_call(
        _inproj_kernel,
        out_shape=jax.ShapeDtypeStruct((rows, N_IN), F32),
        grid=(2, rows // ROW_TILE),
        in_specs=[pl.BlockSpec((ROW_TILE, D_MODEL), lambda j, i: (i, 0)),
                  pl.BlockSpec((1, 3, D_MODEL), lambda j, i: (i // tps, 0, 0)),
                  pl.BlockSpec((1, D_MODEL), lambda j, i: (0, 0)),
                  pl.BlockSpec((D_MODEL, half), lambda j, i: (0, j))],
        out_specs=pl.BlockSpec((ROW_TILE, half), lambda j, i: (i, j)),
        compiler_params=_params(("parallel", "parallel")),
        name="inproj",
    )(x, mod, norm_g.reshape(1, D_MODEL), w_in_bf)


def _neighbours(cur, prev_blk, next_blk, first, last):
    rows = cur.shape[0]
    rid = lax.broadcasted_iota(jnp.int32, cur.shape, 0)
    prow = jnp.where(first, 0.0, prev_blk[HALO - 1:HALO, :])
    nrow = jnp.where(last, 0.0, next_blk[0:1, :])
    up = jnp.where(rid == 0, prow, pltpu.roll(cur, 1, 0))
    dn = jnp.where(rid == rows - 1, nrow, pltpu.roll(cur, rows - 1, 0))
    return up, dn


def _halo_specs(width, col, rows):
    per = ROW_TILE // HALO
    nblk = rows // HALO
    cb = col // width
    return [pl.BlockSpec((ROW_TILE, width), lambda i: (i, cb)),
            pl.BlockSpec((HALO, width), lambda i: (jnp.maximum(i * per - 1, 0), cb)),
            pl.BlockSpec((HALO, width), lambda i: (jnp.minimum((i + 1) * per, nblk - 1), cb))]


def _hy_pre_kernel(tps, zc, zp, zn, g_ref, cw, cb, u_ref, x1g_ref):
    i = pl.program_id(0)
    first = (i % tps) == 0
    last = (i % tps) == tps - 1
    z = zc[...]
    up, dn = _neighbours(z, zp[...], zn[...], first, last)
    u = up * cw[0:1, :] + z * cw[1:2, :] + dn * cw[2:3, :] + cb[...]
    v = u[:, :BR_W]
    x1 = u[:, BR_W:2 * BR_W]
    x2 = u[:, 2 * BR_W:]
    u2 = x2 * v
    hi, lo = _split2(u2)
    u_ref[:, :BR_W] = hi
    u_ref[:, BR_W:] = lo
    x1g_ref[...] = x1 * _silu(g_ref[...])


def _hy_pre(z, conv_w, conv_b, tps):
    rows = z.shape[0]
    return pl.pallas_call(
        functools.partial(_hy_pre_kernel, tps),
        out_shape=(jax.ShapeDtypeStruct((rows, 2 * BR_W), BF16),
                   jax.ShapeDtypeStruct((rows, BR_W), F32)),
        grid=(rows // ROW_TILE,),
        in_specs=_halo_specs(3 * BR_W, COL_HY_Z, rows) + [
            pl.BlockSpec((ROW_TILE, BR_W), lambda i: (i, COL_HY_G // BR_W)),
            pl.BlockSpec((3, 3 * BR_W), lambda i: (0, 0)),
            pl.BlockSpec((1, 3 * BR_W), lambda i: (0, 0))],
        out_specs=(pl.BlockSpec((ROW_TILE, 2 * BR_W), lambda i: (i, 0)),
                   pl.BlockSpec((ROW_TILE, BR_W), lambda i: (i, 0))),
        compiler_params=_params(("parallel",)),
        name="hy_pre",
    )(z, z, z, z, conv_w, conv_b.reshape(1, 3 * BR_W))


def _filter_kernel(L, zp, f1, fb1, fr, f2, fb2, f3, o_ref):
    freq = fr[...]
    hid = jnp.sin(freq * (_dot3(zp[...], f1[...]) + fb1[...]))
    hid = jnp.sin(freq * (_dot3(hid, f2[...]) + fb2[...]))
    h = _dot3(hid, f3[...])
    row = lax.broadcasted_iota(jnp.int32, (L, BR_W), 0)
    lane = lax.broadcasted_iota(jnp.int32, (L, BR_W), 1)
    t = row.astype(F32) / float(L - 1)
    delta = HY_MIN_DECAY + lane.astype(F32) * ((HY_MAX_DECAY - HY_MIN_DECAY) / float(BR_W - 1))
    win = jnp.exp(-t * jnp.abs(delta))
    hf = h[:, :BR_W] * win
    hb = jnp.where(row == 0, 0.0, h[:, BR_W:] * win)
    hfh, hfl = _split2(hf)
    hbh, hbl = _split2(hb)
    o_ref[:, 0 * BR_W:1 * BR_W] = hfh
    o_ref[:, 1 * BR_W:2 * BR_W] = hbh
    o_ref[:, 2 * BR_W:3 * BR_W] = hfl
    o_ref[:, 3 * BR_W:4 * BR_W] = hbl


def _hy_filters(L, zpos, f1p, fb1, freq, f2, fb2, f3):
    return pl.pallas_call(
        functools.partial(_filter_kernel, L),
        out_shape=jax.ShapeDtypeStruct((L, 4 * BR_W), BF16),
        compiler_params=_params(None),
        name="hy_filter",
    )(zpos, f1p, fb1.reshape(1, HY_FFN), freq.reshape(1, HY_FFN), f2, fb2.reshape(1, HY_FFN), f3)


def _filter_dft_kernel(tk, L, ch, cl, sh, sl, hfb, hre_ref, him_ref):
    j = pl.program_id(0)
    xh = hfb[:, :2 * BR_W]
    xl = hfb[:, 2 * BR_W:]
    fre = _dot3s(ch[...], cl[...], xh, xl)
    fim = _dot3s(sh[...], sl[...], xh, xl)
    row = lax.broadcasted_iota(jnp.int32, (tk, BR_W), 0) + j * tk
    is0 = row == 0
    scale = jnp.where(is0, 1.0 / (2 * L), 2.0 / (2 * L))
    hre_ref[...] = (fre[:, :BR_W] + fre[:, BR_W:]) * scale
    him_ref[...] = jnp.where(is0, fim[:, :BR_W] + fim[:, BR_W:], fim[:, :BR_W] - fim[:, BR_W:]) * scale


def _filter_dft(L, tk, tabs, hfb):
    ch, cl, sh, sl = tabs[:4]
    tab = pl.BlockSpec((tk, L), lambda j: (j, 0))
    return pl.pallas_call(
        functools.partial(_filter_dft_kernel, tk, L),
        out_shape=(jax.ShapeDtypeStruct((L, BR_W), F32), jax.ShapeDtypeStruct((L, BR_W), F32)),
        grid=(L // tk,),
        in_specs=[tab, tab, tab, tab, pl.BlockSpec((L, 4 * BR_W), lambda j: (0, 0))],
        out_specs=(pl.BlockSpec((tk, BR_W), lambda j: (j, 0)), pl.BlockSpec((tk, BR_W), lambda j: (j, 0))),
        compiler_params=_params(("parallel",)),
        name="hy_filter_dft",
    )(ch, cl, sh, sl, hfb)


def _dft_fwd_kernel(tk, ch, cl, sh, sl, u_ref, hre_ref, him_ref, y_ref):
    j = pl.program_id(1)
    uh = u_ref[:, :BR_W]
    ul = u_ref[:, BR_W:]
    ure = _dot3s(ch[...], cl[...], uh, ul)
    uim = _dot3s(sh[...], sl[...], uh, ul)
    hre = hre_ref[...]
    him = him_ref[...]
    row = lax.broadcasted_iota(jnp.int32, (tk, BR_W), 0) + j * tk
    is0 = row == 0
    yre = jnp.where(is0, ure * hre, ure * hre - uim * him)
    yim = jnp.where(is0, uim * him, ure * him + uim * hre)
    a, b = _split2(yre)
    c, d = _split2(yim)
    y_ref[:, 0 * BR_W:1 * BR_W] = a
    y_ref[:, 1 * BR_W:2 * BR_W] = b
    y_ref[:, 2 * BR_W:3 * BR_W] = c
    y_ref[:, 3 * BR_W:4 * BR_W] = d


def _dft_fwd(B, L, tk, tabs, u2, hre, him):
    ch, cl, sh, sl = tabs[:4]
    nk = L // tk
    tab = pl.BlockSpec((tk, L), lambda b, j: (j, 0))
    hsp = pl.BlockSpec((tk, BR_W), lambda b, j: (j, 0))
    return pl.pallas_call(
        functools.partial(_dft_fwd_kernel, tk),
        out_shape=jax.ShapeDtypeStruct((B * L, 4 * BR_W), BF16),
        grid=(B, nk),
        in_specs=[tab, tab, tab, tab, pl.BlockSpec((L, 2 * BR_W), lambda b, j: (b, 0)), hsp, hsp],
        out_specs=pl.BlockSpec((tk, 4 * BR_W), lambda b, j: (b * nk + j, 0)),
        compiler_params=_params(("parallel", "parallel")),
        name="hy_dft_fwd",
    )(ch, cl, sh, sl, u2, hre, him)


def _dft_inv_kernel(ch, cl, sth, stl, y_ref, u_ref, x1g_ref, bias_ref, o_ref):
    conv = (_dot3s(ch[...], cl[...], y_ref[:, 0 * BR_W:1 * BR_W], y_ref[:, 1 * BR_W:2 * BR_W])
            + _dot3s(sth[...], stl[...], y_ref[:, 2 * BR_W:3 * BR_W], y_ref[:, 3 * BR_W:4 * BR_W]))
    u2 = u_ref[:, :BR_W].astype(F32) + u_ref[:, BR_W:].astype(F32)
    o_ref[...] = (conv + u2 * bias_ref[...]) * x1g_ref[...]


def _dft_inv(B, L, tm, tabs, y4, u2, x1g, bias):
    ch, cl, _, _, sth, stl = tabs
    nt = L // tm
    tab = pl.BlockSpec((tm, L), lambda b, i: (i, 0))
    return pl.pallas_call(
        _dft_inv_kernel,
        out_shape=jax.ShapeDtypeStruct((B * L, BR_W), F32),
        grid=(B, nt),
        in_specs=[tab, tab, tab, tab,
                  pl.BlockSpec((L, 4 * BR_W), lambda b, i: (b, 0)),
                  pl.BlockSpec((tm, 2 * BR_W), lambda b, i: (b * nt + i, 0)),
                  pl.BlockSpec((tm, BR_W), lambda b, i: (b * nt + i, 0)),
                  pl.BlockSpec((1, BR_W), lambda b, i: (0, 0))],
        out_specs=pl.BlockSpec((tm, BR_W), lambda b, i: (b * nt + i, 0)),
        compiler_params=_params(("parallel", "parallel")),
        name="hy_dft_inv",
    )(ch, cl, sth, stl, y4, u2, x1g, bias.reshape(1, BR_W))


def _dft_tables(L):
    n = 2 * L
    k = jnp.arange(L, dtype=jnp.int32)
    m = (k[:, None] * k[None, :]) % n
    ang = m.astype(F32) * (2.0 * math.pi / n)
    c = jnp.cos(ang)
    s = -jnp.sin(ang)
    alt = jnp.where(k % 2 == 0, 1.0, -1.0).astype(F32)
    s = jnp.where((k == 0)[:, None], alt[None, :], s)
    out = []
    for t in (c, s, s.T):
        hi = t.astype(BF16)
        lo = (t - hi.astype(F32)).astype(BF16)
        out += [hi, lo]
    return tuple(out)


def _hy_positions(L):
    t = jnp.linspace(0.0, 1.0, L, dtype=F32)[:, None]
    w = 2.0 * math.pi * jnp.arange(L, dtype=F32)[:, None] / L
    bands = jnp.linspace(1e-4, HY_BANDS - 1, HY_BANDS, dtype=F32)[None, :]
    z = jnp.concatenate([t, jnp.cos(bands * w), -jnp.sin(bands * w)], axis=-1)
    return jnp.pad(z, ((0, 0), (0, LANES - HY_EMB)))


def _rw_prep_kernel(tps, rc, rp, rn, kc, kp, kn, vc, vp, vn, lc, lp, ln,
                    mu_r, mu_k, mu_v, mu_l, kkw, ka, rk, w0, a0, w2, a2, ones_bd,
                    r_out, nkk_out, v_out, bonus_out, wf, bf, kdf, wb, bb, kdb):
    i = pl.program_id(0)
    first = (i % tps) == 0
    last = (i % tps) == tps - 1

    def shifted(cur_ref, prev_ref, next_ref, mu_ref):
        cur = cur_ref[...]
        up, dn = _neighbours(cur, prev_ref[...], next_ref[...], first, last)
        return cur + mu_ref[...] * (0.5 * (up + dn) - cur)

    r = shifted(rc, rp, rn, mu_r)
    k = shifted(kc, kp, kn, mu_k)
    v = shifted(vc, vp, vn, mu_v)
    lora = shifted(lc, lp, ln, mu_l)
    ones = ones_bd[...]

    kk = k * kkw[...]
    kk = kk * lax.rsqrt(_segsum(kk * kk, ones) + 1e-12)
    r_out[...] = r
    v_out[...] = v
    nkk_out[...] = -kk

    kd_sum = None
    outs = ((wf, bf, kdf), (wb, bb, kdb))
    for d in range(2):
        wl = lora[:, d * W_LORA:(d + 1) * W_LORA]
        al = lora[:, 2 * W_LORA + d * A_LORA:2 * W_LORA + (d + 1) * A_LORA]
        w = jnp.exp(-RW_DECAY_SCALE * _sigmoid(w0[d:d + 1, :] + _dot3(jnp.tanh(wl), w2[d])))
        a = _sigmoid(a0[d:d + 1, :] + _dot3(al, a2[d]))
        kd = k * (1.0 + (a - 1.0) * ka[...])
        outs[d][0][...] = w
        outs[d][1][...] = kk * a
        outs[d][2][...] = kd
        kd_sum = kd if kd_sum is None else kd_sum + kd
    bonus_out[...] = _segsum(r * (0.5 * kd_sum) * rk[...], ones) * v


def _rw_prep(z, p, ones_bd, tps):
    rows = z.shape[0]
    mu = p["rw_mu"]
    vec = lambda a: a.reshape(1, -1)
    full = lambda a: pl.BlockSpec(a.shape, lambda i: (0,) * a.ndim)
    smalls = [vec(mu[0:512]), vec(mu[512:1024]), vec(mu[1024:1536]), vec(mu[1536:1792]),
              vec(p["rw_kk"]), vec(p["rw_ka"]), vec(p["rw_rk"]), p["rw_w0"], p["rw_a0"],
              p["rw_w2"], p["rw_a2"], ones_bd]
    tile = pl.BlockSpec((ROW_TILE, BR_W), lambda i: (i, 0))
    return pl.pallas_call(
        functools.partial(_rw_prep_kernel, tps),
        out_shape=tuple(jax.ShapeDtypeStruct((rows, BR_W), F32) for _ in range(10)),
        grid=(rows // ROW_TILE,),
        in_specs=(_halo_specs(BR_W, COL_RW_R, rows) + _halo_specs(BR_W, COL_RW_K, rows)
                  + _halo_specs(BR_W, COL_RW_V, rows) + _halo_specs(2 * LANES, COL_LORA, rows)
                  + [full(a) for a in smalls]),
        out_specs=tuple(tile for _ in range(10)),
        compiler_params=_params(("parallel",)),
        name="rw_prep",
    )(*([z] * 12), *smalls)


def _scan_kernel(nc, bb, r_f, nk_f, w_f, b_f, kd_f, vt_f, r_b, nk_b, w_b, b_b, kd_b, vt_b,
                 s0, ones2, yt_f, yt_b, sfin, state, lhs_a, lhs_b):
    c = pl.program_id(1)
    npair = RW_H // 2
    nsub = SCAN_CHUNK // SUB
    chains = [(bi, d, p) for bi in range(bb) for d in range(2) for p in range(npair)]

    @pl.when(c == 0)
    def _():
        for ci, (bi, d, p) in enumerate(chains):
            state[ci] = s0[bi, d, p]

    g2 = ones2[...]
    lane_full = lax.broadcasted_iota(jnp.int32, (RW_N, LANES), 1)
    lane = lane_full & (SUB - 1)
    lane_head = lane_full & SUB
    dirs = ((r_f, nk_f, w_f, b_f, kd_f, vt_f, yt_f), (r_b, nk_b, w_b, b_b, kd_b, vt_b, yt_b))

    for sub in range(nsub):
        subs = (sub, nsub - 1 - sub)
        for bi, d, p in chains:
            dirs[d][6][bi, subs[d], p] = jnp.zeros((RW_N, LANES), F32)

        def body(g, carry):
            bases = (pl.multiple_of(g * HALO, HALO), pl.multiple_of((SUB // HALO - 1 - g) * HALO, HALO))
            for jj in range(HALO):
                js = (jj, HALO - 1 - jj)

                def row(ref, bi, d, p):
                    blk = ref[bi, pl.ds(subs[d] * SUB + bases[d], HALO), pl.ds(p * LANES, LANES)]
                    return blk[js[d]:js[d] + 1, :]

                for ci, (bi, d, p) in enumerate(chains):
                    zh, zl = _split2(state[ci] * row(dirs[d][1], bi, d, p))
                    lhs_a[ci * RW_N:(ci + 1) * RW_N, :LANES] = zh
                    lhs_a[ci * RW_N:(ci + 1) * RW_N, LANES:] = zl
                res = _dot(lhs_a[...], g2)
                for ci, (bi, d, p) in enumerate(chains):
                    tt = bases[d] + js[d]
                    vb = jnp.take_along_axis(dirs[d][5][bi, subs[d], p], lane_head + tt, axis=1)
                    st = (state[ci] * row(dirs[d][2], bi, d, p)
                          + res[ci * RW_N:(ci + 1) * RW_N] * row(dirs[d][3], bi, d, p)
                          + vb * row(dirs[d][4], bi, d, p))
                    state[ci] = st
                    yh, yl = _split2(st * row(dirs[d][0], bi, d, p))
                    lhs_b[ci * RW_N:(ci + 1) * RW_N, :LANES] = yh
                    lhs_b[ci * RW_N:(ci + 1) * RW_N, LANES:] = yl
                res = _dot(lhs_b[...], g2)
                for ci, (bi, d, p) in enumerate(chains):
                    y_ref = dirs[d][6]
                    sel = lane == bases[d] + js[d]
                    y_ref[bi, subs[d], p] = jnp.where(sel, res[ci * RW_N:(ci + 1) * RW_N], y_ref[bi, subs[d], p])
            return carry

        lax.fori_loop(0, SUB // HALO, body, 0)

    @pl.when(c == nc - 1)
    def _():
        for ci, (bi, d, p) in enumerate(chains):
            sfin[bi, d, p] = state[ci]


def _rw_scan(B, L, seqs, vt, s0, ones2):
    r, nkk, wf, bf, kdf, wb, bwb, kdb = [a.reshape(B, L, BR_W) for a in seqs]
    bb = SCAN_BATCH
    nc = L // SCAN_CHUNK
    nsub = SCAN_CHUNK // SUB
    npair = RW_H // 2
    nch = bb * 2 * npair
    fwd = pl.BlockSpec((bb, SCAN_CHUNK, BR_W), lambda b, c: (b, c, 0))
    bwd = pl.BlockSpec((bb, SCAN_CHUNK, BR_W), lambda b, c: (b, nc - 1 - c, 0))
    tile = (bb, nsub, npair, RW_N, LANES)
    vfwd = pl.BlockSpec(tile, lambda b, c: (b, c, 0, 0, 0))
    vbwd = pl.BlockSpec(tile, lambda b, c: (b, nc - 1 - c, 0, 0, 0))
    sspec = pl.BlockSpec((bb, 2, npair, RW_N, LANES), lambda b, c: (b, 0, 0, 0, 0))
    yshape = jax.ShapeDtypeStruct((B, L // SUB, npair, RW_N, LANES), F32)
    return pl.pallas_call(
        functools.partial(_scan_kernel, nc, bb),
        out_shape=(yshape, yshape, jax.ShapeDtypeStruct((B, 2, npair, RW_N, LANES), F32)),
        grid=(B // bb, nc),
        in_specs=[fwd, fwd, fwd, fwd, fwd, vfwd, bwd, bwd, bwd, bwd, bwd, vbwd, sspec,
                  pl.BlockSpec((2 * LANES, LANES), lambda b, c: (0, 0))],
        out_specs=(vfwd, vbwd, sspec),
        scratch_shapes=[pltpu.VMEM((nch, RW_N, LANES), F32),
                        pltpu.VMEM((nch * RW_N, 2 * LANES), BF16),
                        pltpu.VMEM((nch * RW_N, 2 * LANES), BF16)],
        compiler_params=_params(("parallel", "arbitrary")),
        name="rw_scan",
    )(r, nkk, wf, bf, kdf, vt, r, nkk, wb, bwb, kdb, vt, s0, ones2)


def _rw_post_kernel(yf, yb, bonus, g_ref, lnw, lnb, ones_bd, o_ref):
    ones = ones_bd[...]
    y = yf[...] + yb[...]
    mu = _segsum(y, ones) * (1.0 / RW_N)
    yc = y - mu
    var = _segsum(yc * yc, ones) * (1.0 / RW_N)
    y = yc * lax.rsqrt(var + RW_GN_EPS) * lnw[...] + lnb[...]
    o_ref[...] = (y + bonus[...]) * _silu(g_ref[...])


def _rw_post(yf, yb, bonus, z, ln_w, ln_b, ones_bd):
    rows = yf.shape[0]
    tile = pl.BlockSpec((ROW_TILE, BR_W), lambda i: (i, 0))
    vec = pl.BlockSpec((1, BR_W), lambda i: (0, 0))
    return pl.pallas_call(
        _rw_post_kernel,
        out_shape=jax.ShapeDtypeStruct((rows, BR_W), F32),
        grid=(rows // ROW_TILE,),
        in_specs=[tile, tile, tile, pl.BlockSpec((ROW_TILE, BR_W), lambda i: (i, COL_RW_G // BR_W)),
                  vec, vec, pl.BlockSpec((BR_W, BR_W), lambda i: (0, 0))],
        out_specs=tile,
        compiler_params=_params(("parallel",)),
        name="rw_post",
    )(yf, yb, bonus, z, ln_w.reshape(1, BR_W), ln_b.reshape(1, BR_W), ones_bd)


def _rope(x, cos, sin_signed):
    lane = lax.broadcasted_iota(jnp.int32, x.shape, 1)
    low = (lane & (ROPE_AXIS - 1)) < (ROPE_AXIS // 2)
    width = x.shape[1]
    partner = jnp.where(low, pltpu.roll(x, width - ROPE_AXIS // 2, 1), pltpu.roll(x, ROPE_AXIS // 2, 1))
    return x * cos + partner * sin_signed


def _da_prep_kernel(rope, q_ref, k_ref, gq, gk, ones_bd, *rest):
    if rope:
        cos_ref, sin_ref, q_out, k_out, kn_out = rest
    else:
        q_out, k_out, kn_out = rest
    ones = ones_bd[...]

    def norm(x, g):
        ms = _segsum(x * x, ones) * (1.0 / DA_DH)
        return x * lax.rsqrt(ms + NORM_EPS) * g

    qn = norm(q_ref[...], gq[...])
    kn = norm(k_ref[...], gk[...])
    kn_out[...] = kn
    if rope:
        qn = _rope(qn, cos_ref[...], sin_ref[...])
        kn = _rope(kn, cos_ref[...], sin_ref[...])
    k_out[...] = kn.astype(BF16)
    qn = qn * (DA_DH ** -0.5)
    lane = lax.broadcasted_iota(jnp.int32, (qn.shape[0], LANES), 1)
    for h in range(DA_H):
        qh = qn[:, h * LANES:(h + 1) * LANES]
        q_out[:, (2 * h) * LANES:(2 * h + 1) * LANES] = jnp.where(lane < DA_DH, qh, 0.0).astype(BF16)
        q_out[:, (2 * h + 1) * LANES:(2 * h + 2) * LANES] = jnp.where(lane >= DA_DH, qh, 0.0).astype(BF16)


def _da_prep(z, gq, gk, ones_bd, tps, rope_tabs):
    rows = z.shape[0]
    rope = rope_tabs is not None
    tile = lambda col: pl.BlockSpec((ROW_TILE, BR_W), lambda i: (i, col // BR_W))
    vec = pl.BlockSpec((1, BR_W), lambda i: (0, 0))
    in_specs = [tile(COL_DA_Q), tile(COL_DA_K), vec, vec, pl.BlockSpec((BR_W, BR_W), lambda i: (0, 0))]
    args = [z, z, jnp.tile(gq, 2 * DA_H).reshape(1, BR_W), jnp.tile(gk, 2 * DA_H).reshape(1, BR_W), ones_bd]
    if rope:
        tab = pl.BlockSpec((ROW_TILE, BR_W), lambda i: (i % tps, 0))
        in_specs += [tab, tab]
        args += list(rope_tabs)
    return pl.pallas_call(
        functools.partial(_da_prep_kernel, rope),
        out_shape=(jax.ShapeDtypeStruct((rows, 2 * BR_W), BF16),
                   jax.ShapeDtypeStruct((rows, BR_W), BF16),
                   jax.ShapeDtypeStruct((rows, BR_W), F32)),
        grid=(rows // ROW_TILE,),
        in_specs=in_specs,
        out_specs=(pl.BlockSpec((ROW_TILE, 2 * BR_W), lambda i: (i, 0)),
                   pl.BlockSpec((ROW_TILE, BR_W), lambda i: (i, 0)),
                   pl.BlockSpec((ROW_TILE, BR_W), lambda i: (i, 0))),
        compiler_params=_params(("parallel",)),
        name="da_prep",
    )(*args)


def _rope_tables(L):
    half = ROPE_AXIS // 2
    lane = jnp.arange(BR_W)
    j = lane % DA_DH
    use_col = (j // ROPE_AXIS) == 1
    idx = (j % ROPE_AXIS) % half
    freqs = ROPE_BASE ** (-idx.astype(F32) / half)
    t = jnp.arange(L)
    pos = jnp.where(use_col[None, :], (t % GRID_W)[:, None], (t // GRID_W)[:, None]).astype(F32)
    ang = pos * freqs[None, :]
    sign = jnp.where((j % ROPE_AXIS) < half, -1.0, 1.0).astype(F32)
    return jnp.cos(ang), jnp.sin(ang) * sign[None, :]


def _da_attn_kernel(lam_init, q_ref, k_ref, v_ref, g_ref, gsub, lq1, lk1, lq2, lk2, o_ref):
    lam = (jnp.exp(jnp.sum(lq1[...] * lk1[...], axis=-1, keepdims=True))
           - jnp.exp(jnp.sum(lq2[...] * lk2[...], axis=-1, keepdims=True)) + lam_init)
    nt = (((1,), (1,)), ((), ()))
    for h in range(DA_H):
        kh = k_ref[0, :, h * LANES:(h + 1) * LANES]
        vh = v_ref[0, :, h * LANES:(h + 1) * LANES]
        outs = []
        for m in range(2):
            q = q_ref[:, (2 * h + m) * LANES:(2 * h + m + 1) * LANES]
            s = lax.dot_general(q, kh, nt, preferred_element_type=F32)
            e = jnp.exp(s - jnp.max(s, axis=-1, keepdims=True))
            den = jnp.sum(e, axis=-1, keepdims=True)
            outs.append(_dot(e.astype(BF16), vh) / den)
        o = outs[0] - lam * outs[1]
        o = o * lax.rsqrt(jnp.mean(o * o, axis=-1, keepdims=True) + 1e-5) * gsub[...] * (1.0 - lam_init)
        o_ref[:, h * LANES:(h + 1) * LANES] = o * _silu(g_ref[:, h * LANES:(h + 1) * LANES])


def _da_attn(B, L, lam_init, q2, kcat, vcat, z, p):
    lk = kcat.shape[1]
    nt = L // ROW_TILE
    vec = lambda a: a.reshape(1, -1)
    small = lambda n: pl.BlockSpec((1, n), lambda b, i: (0, 0))
    return pl.pallas_call(
        functools.partial(_da_attn_kernel, lam_init),
        out_shape=jax.ShapeDtypeStruct((B * L, BR_W), F32),
        grid=(B, nt),
        in_specs=[pl.BlockSpec((ROW_TILE, 2 * BR_W), lambda b, i: (b * nt + i, 0)),
                  pl.BlockSpec((1, lk, BR_W), lambda b, i: (b, 0, 0)),
                  pl.BlockSpec((1, lk, BR_W), lambda b, i: (b, 0, 0)),
                  pl.BlockSpec((ROW_TILE, BR_W), lambda b, i: (b * nt + i, COL_DA_G // BR_W)),
                  small(2 * DA_DH), small(DA_DH), small(DA_DH), small(DA_DH), small(DA_DH)],
        out_specs=pl.BlockSpec((ROW_TILE, BR_W), lambda b, i: (b * nt + i, 0)),
        compiler_params=_params(("parallel", "parallel")),
        name="da_attn",
    )(q2, kcat, vcat, z, vec(p["da_gsub"]), vec(p["da_lq1"]), vec(p["da_lk1"]),
      vec(p["da_lq2"]), vec(p["da_lk2"]))


def _merge_kernel(ya, yb, yc, mga, mgb, mgc, x_ref, mod_ref, wbr, wout, o_ref):
    merged = (_sigmoid(mga[...]) * _dot(ya[...].astype(BF16), wbr[0])
              + _sigmoid(mgb[...]) * _dot(yb[...].astype(BF16), wbr[1])
              + _sigmoid(mgc[...]) * _dot(yc[...].astype(BF16), wbr[2]))
    out = _dot(merged.astype(BF16), wout[...])
    o_ref[...] = x_ref[...] + mod_ref[0][2:3] * out


def _merge(ya, yb, yc, z, x, mod, wbr_bf, wout_bf, tps):
    rows = x.shape[0]
    tile = pl.BlockSpec((ROW_TILE, BR_W), lambda i: (i, 0))
    mg = lambda n: pl.BlockSpec((ROW_TILE, D_MODEL), lambda i: (i, COL_MG // D_MODEL + n))
    return pl.pallas_call(
        _merge_kernel,
        out_shape=jax.ShapeDtypeStruct((rows, D_MODEL), F32),
        grid=(rows // ROW_TILE,),
        in_specs=[tile, tile, tile, mg(0), mg(1), mg(2),
                  pl.BlockSpec((ROW_TILE, D_MODEL), lambda i: (i, 0)),
                  pl.BlockSpec((1, 3, D_MODEL), lambda i: (i // tps, 0, 0)),
                  pl.BlockSpec((3, BR_W, D_MODEL), lambda i: (0, 0, 0)),
                  pl.BlockSpec((D_MODEL, D_MODEL), lambda i: (0, 0))],
        out_specs=pl.BlockSpec((ROW_TILE, D_MODEL), lambda i: (i, 0)),
        compiler_params=_params(("parallel",)),
        name="merge_out",
    )(ya, yb, yc, z, z, z, x, mod, wbr_bf, wout_bf)


def _to_value_tiles(v, B, L):
    t = v.reshape(B, L // SUB, SUB, RW_H // 2, 2, RW_N)
    return t.transpose(0, 1, 3, 5, 4, 2).reshape(B, L // SUB, RW_H // 2, RW_N, 2 * SUB)


def _from_value_tiles(yt, B, L):
    t = yt.reshape(B, L // SUB, RW_H // 2, RW_N, 2, SUB)
    return t.transpose(0, 1, 5, 2, 4, 3).reshape(B * L, BR_W)


def _state_to_pairs(s):
    B = s.shape[0]
    t = s.reshape(B, 2, RW_H // 2, 2, RW_N, RW_N)
    return t.transpose(0, 1, 2, 4, 3, 5).reshape(B, 2, RW_H // 2, RW_N, 2 * RW_N)


def _state_from_pairs(s):
    B = s.shape[0]
    t = s.reshape(B, 2, RW_H // 2, RW_N, 2, RW_N)
    return t.transpose(0, 1, 2, 4, 3, 5).reshape(B, 2, RW_H, RW_N, RW_N)


def _permute_w_in(w):
    hy_z, hy_g = w[:, 0:1536], w[:, 1536:2048]
    rw_rkv, rw_lora, rw_g = w[:, 2048:3584], w[:, 3584:3840], w[:, 3840:4352]
    da = w[:, 4352:6400]
    mg = w[:, 6400:9472]
    return jnp.concatenate([hy_z, hy_g, rw_rkv, rw_g, da, mg, rw_lora], axis=1)


def _layer(x, mod, p, consts, B, L, lam_init, ctx):
    tps = L // ROW_TILE
    tabs, zpos, rope_tabs, ones_bd, ones2 = consts
    z = _inproj(x, mod, p["norm_g"], p["w_in_bf"], tps)

    u2, x1g = _hy_pre(z, p["hy_conv_w"], p["hy_conv_b"], tps)
    tk = min(L, 256)
    hfb = _hy_filters(L, zpos, p["hy_f1p"], p["hy_fb1"], p["hy_freq"], p["hy_f2"], p["hy_fb2"], p["hy_f3"])
    hre, him = _filter_dft(L, tk, tabs, hfb)
    y4 = _dft_fwd(B, L, tk, tabs, u2, hre, him)
    y_a = _dft_inv(B, L, min(L, 512), tabs, y4, u2, x1g, p["hy_bias"])

    r, nkk, v, bonus, wf, bf, kdf, wb, bb, kdb = _rw_prep(z, p, ones_bd, tps)
    npair = RW_H // 2
    if ctx is None:
        s0 = jnp.zeros((B, 2, npair, RW_N, LANES), F32)
    else:
        s0 = _state_to_pairs(ctx[2])
    ytf, ytb, sfin = _rw_scan(B, L, (r, nkk, wf, bf, kdf, wb, bb, kdb), _to_value_tiles(v, B, L),
                              s0, ones2)
    y_b = _rw_post(_from_value_tiles(ytf, B, L), _from_value_tiles(ytb, B, L), bonus, z,
                   p["rw_ln_w"], p["rw_ln_b"], ones_bd)

    q2, k_bf, kn = _da_prep(z, p["da_gq"], p["da_gk"], ones_bd, tps, rope_tabs if ctx is not None else None)
    v_da = z[:, COL_DA_V:COL_DA_V + BR_W]
    k3 = k_bf.reshape(B, L, BR_W)
    v3 = v_da.astype(BF16).reshape(B, L, BR_W)
    if ctx is not None:
        k3 = jnp.concatenate([ctx[0].reshape(B, -1, BR_W).astype(BF16), k3], axis=1)
        v3 = jnp.concatenate([ctx[1].reshape(B, -1, BR_W).astype(BF16), v3], axis=1)
    y_c = _da_attn(B, L, lam_init, q2, k3, v3, z, p)

    x_new = _merge(y_a, y_b, y_c, z, x, mod, p["w_br_bf"], p["w_out_bf"], tps)
    return x_new, (kn, v_da, sfin)


def kernel(x_prompt, x_sample, cache_k, cache_v, state_rwkv, c, c_ctx, norm_g, w_ada, b_ada, w_in, hy_conv_w, hy_conv_b, hy_f1, hy_fb1, hy_freq, hy_f2, hy_fb2, hy_f3, hy_bias, rw_mu, rw_w0, rw_w2, rw_a0, rw_a2, rw_kk, rw_ka, rw_rk, rw_ln_w, rw_ln_b, da_gq, da_gk, da_lq1, da_lk1, da_lq2, da_lk2, da_gsub, w_br, w_out):
    Bp, Lp, _ = x_prompt.shape
    Bs, Ls, _ = x_sample.shape

    pad = (-(Bp + Bs)) % 8
    c_all = jnp.concatenate([jnp.broadcast_to(c_ctx, (Bp, D_MODEL)), c, jnp.zeros((pad, D_MODEL), F32)], axis=0)
    mod_all = _modulation(c_all, w_ada, b_ada)
    mod_p = mod_all[:, :Bp].reshape(DEPTH, Bp, 3, D_MODEL)
    mod_s = mod_all[:, Bp:Bp + Bs].reshape(DEPTH, Bs, 3, D_MODEL)

    seg = jnp.arange(BR_W) // RW_N
    ones_bd = (seg[:, None] == seg[None, :]).astype(BF16)
    pair_seg = jnp.arange(LANES) // RW_N
    g = (pair_seg[:, None] == pair_seg[None, :]).astype(BF16)
    ones2 = jnp.concatenate([g, g], axis=0)

    consts = {}
    for L in sorted({Lp, Ls}):
        consts[L] = (_dft_tables(L), _hy_positions(L), _rope_tables(L), ones_bd, ones2)

    def layer_params(l):
        return dict(norm_g=norm_g[l], w_in_bf=_permute_w_in(w_in[l]).astype(BF16),
                    hy_conv_w=hy_conv_w[l], hy_conv_b=hy_conv_b[l],
                    hy_f1p=jnp.pad(hy_f1[l], ((0, LANES - HY_EMB), (0, 0))), hy_fb1=hy_fb1[l],
                    hy_freq=hy_freq[l], hy_f2=hy_f2[l], hy_fb2=hy_fb2[l], hy_f3=hy_f3[l],
                    hy_bias=hy_bias[l], rw_mu=rw_mu[l], rw_w0=rw_w0[l], rw_w2=rw_w2[l],
                    rw_a0=rw_a0[l], rw_a2=rw_a2[l], rw_kk=rw_kk[l], rw_ka=rw_ka[l], rw_rk=rw_rk[l],
                    rw_ln_w=rw_ln_w[l], rw_ln_b=rw_ln_b[l], da_gq=da_gq[l], da_gk=da_gk[l],
                    da_lq1=da_lq1[l], da_lk1=da_lk1[l], da_lq2=da_lq2[l], da_lk2=da_lk2[l],
                    da_gsub=da_gsub[l], w_br_bf=w_br[l].astype(BF16), w_out_bf=w_out[l].astype(BF16))

    params = [layer_params(l) for l in range(DEPTH)]

    xp = x_prompt.reshape(Bp * Lp, D_MODEL)
    new_k, new_v, new_s = [], [], []
    for l in range(DEPTH):
        lam_init = 0.8 - 0.6 * math.exp(-0.3 * l)
        xp, (k_l, v_l, s_l) = _layer(xp, mod_p[l], params[l], consts[Lp], Bp, Lp, lam_init, None)
        new_k.append(k_l.reshape(Bp, Lp, DA_H, 2, DA_DH))
        new_v.append(v_l.reshape(Bp, Lp, DA_H, 2 * DA_DH))
        new_s.append(_state_from_pairs(s_l))
    new_cache_k = jnp.stack(new_k, axis=1)
    new_cache_v = jnp.stack(new_v, axis=1)
    new_state = jnp.stack(new_s, axis=1)

    xs = x_sample.reshape(Bs * Ls, D_MODEL)
    for l in range(DEPTH):
        lam_init = 0.8 - 0.6 * math.exp(-0.3 * l)
        ctx = (cache_k[:, l], cache_v[:, l], state_rwkv[:, l])
        xs, _ = _layer(xs, mod_s[l], params[l], consts[Ls], Bs, Ls, lam_init, ctx)

    return (xp.reshape(Bp, Lp, D_MODEL), xs.reshape(Bs, Ls, D_MODEL), new_cache_k, new_cache_v, new_state)
```

```python
import functools
import math

import jax
import jax.numpy as jnp
from jax import lax
from jax.experimental import pallas as pl
from jax.experimental.pallas import tpu as pltpu

F32 = jnp.float32
BF16 = jnp.bfloat16

D_MODEL = 1024
DEPTH = 4
GRID_W = 64
BR_W = 512
HY_EMB = 33
HY_BANDS = (HY_EMB - 1) // 2
HY_FFN = 64
HY_MAX_DECAY = math.log(1e-2) / 0.3
HY_MIN_DECAY = math.log(1e-2) / 1.5
RW_N = 64
RW_H = BR_W // RW_N
W_LORA = 64
A_LORA = 64
RW_DECAY_SCALE = math.exp(-0.5)
RW_GN_EPS = 64e-5
DA_DH = 64
DA_H = BR_W // (2 * DA_DH)
ROPE_AXIS = DA_DH // 2
ROPE_BASE = 10000.0
NORM_EPS = 1e-6
N_IN = 9472

LANES = 128
ROW_TILE = 256
HALO = 8
SCAN_CHUNK = 128
SCAN_BATCH = 2
SUB = 64
VMEM_LIMIT = 52 * 1024 * 1024

COL_HY_Z = 0
COL_HY_G = 1536
COL_RW_R = 2048
COL_RW_K = 2560
COL_RW_V = 3072
COL_RW_G = 3584
COL_DA_Q = 4096
COL_DA_K = 4608
COL_DA_V = 5120
COL_DA_G = 5632
COL_MG = 6144
COL_LORA = 9216


def _sigmoid(x):
    return 1.0 / (1.0 + jnp.exp(-x))


def _silu(x):
    return x * _sigmoid(x)


def _dot(a, b):
    return jnp.dot(a, b, preferred_element_type=F32)


def _split2(x):
    hi = x.astype(BF16)
    lo = (x - hi.astype(F32)).astype(BF16)
    return hi, lo


def _split3(x):
    hi = x.astype(BF16)
    r1 = x - hi.astype(F32)
    mid = r1.astype(BF16)
    lo = (r1 - mid.astype(F32)).astype(BF16)
    return hi, mid, lo


def _dot3(a, b):
    ah, al = _split2(a)
    bh, bl = _split2(b)
    return _dot(ah, bh) + _dot(ah, bl) + _dot(al, bh)


def _dot3s(ah, al, bh, bl):
    return _dot(ah, bh) + _dot(ah, bl) + _dot(al, bh)


def _segsum(x, ones_bd):
    hi, mid, lo = _split3(x)
    return _dot(hi, ones_bd) + _dot(mid, ones_bd) + _dot(lo, ones_bd)


def _params(sem, vmem=VMEM_LIMIT):
    return pltpu.CompilerParams(dimension_semantics=sem, vmem_limit_bytes=vmem)


def _mod_kernel(c_ref, w_ref, b_ref, o_ref):
    c = c_ref[...]
    o_ref[0] = _dot3(_silu(c), w_ref[0]) + b_ref[0]


def _modulation(c_all, w_ada, b_ada):
    rows = c_all.shape[0]
    nt = 3 * D_MODEL // 1024
    return pl.pallas_call(
        _mod_kernel,
        out_shape=jax.ShapeDtypeStruct((DEPTH, rows, 3 * D_MODEL), F32),
        grid=(DEPTH, nt),
        in_specs=[pl.BlockSpec((rows, D_MODEL), lambda l, j: (0, 0)),
                  pl.BlockSpec((1, D_MODEL, 1024), lambda l, j: (l, 0, j)),
                  pl.BlockSpec((1, 1, 1024), lambda l, j: (l, 0, j))],
        out_specs=pl.BlockSpec((1, rows, 1024), lambda l, j: (l, 0, j)),
        compiler_params=_params(("parallel", "parallel")),
        name="adaln_mod",
    )(c_all, w_ada, b_ada.reshape(DEPTH, 1, 3 * D_MODEL))


def _inproj_kernel(x_ref, mod_ref, g_ref, w_ref, o_ref):
    x = x_ref[...]
    ms = jnp.mean(x * x, axis=-1, keepdims=True)
    y = x * lax.rsqrt(ms + NORM_EPS) * g_ref[...]
    mod = mod_ref[0]
    h = y * (1.0 + mod[1:2]) + mod[0:1]
    o_ref[...] = _dot(h.astype(BF16), w_ref[...])


def _inproj(x, mod, norm_g, w_in_bf, tps):
    rows = x.shape[0]
    half = N_IN // 2
    return pl.pallas_call(
        _inproj_kernel,
        out_shape=jax.ShapeDtypeStruct((rows, N_IN), F32),
        grid=(2, rows // ROW_TILE),
        in_specs=[pl.BlockSpec((ROW_TILE, D_MODEL), lambda j, i: (i, 0)),
                  pl.BlockSpec((1, 3, D_MODEL), lambda j, i: (i // tps, 0, 0)),
                  pl.BlockSpec((1, D_MODEL), lambda j, i: (0, 0)),
                  pl.BlockSpec((D_MODEL, half), lambda j, i: (0, j))],
        out_specs=pl.BlockSpec((ROW_TILE, half), lambda j, i: (i, j)),
        compiler_params=_params(("parallel", "parallel")),
        name="inproj",
    )(x, mod, norm_g.reshape(1, D_MODEL), w_in_bf)


def _neighbours(cur, prev_blk, next_blk, first, last):
    rows = cur.shape[0]
    rid = lax.broadcasted_iota(jnp.int32, cur.shape, 0)
    prow = jnp.where(first, 0.0, prev_blk[HALO - 1:HALO, :])
    nrow = jnp.where(last, 0.0, next_blk[0:1, :])
    up = jnp.where(rid == 0, prow, pltpu.roll(cur, 1, 0))
    dn = jnp.where(rid == rows - 1, nrow, pltpu.roll(cur, rows - 1, 0))
    return up, dn


def _halo_specs(width, col, rows):
    per = ROW_TILE // HALO
    nblk = rows // HALO
    cb = col // width
    return [pl.BlockSpec((ROW_TILE, width), lambda i: (i, cb)),
            pl.BlockSpec((HALO, width), lambda i: (jnp.maximum(i * per - 1, 0), cb)),
            pl.BlockSpec((HALO, width), lambda i: (jnp.minimum((i + 1) * per, nblk - 1), cb))]


def _hy_pre_kernel(tps, zc, zp, zn, g_ref, cw, cb, u_ref, x1g_ref):
    i = pl.program_id(0)
    first = (i % tps) == 0
    last = (i % tps) == tps - 1
    z = zc[...]
    up, dn = _neighbours(z, zp[...], zn[...], first, last)
    u = up * cw[0:1, :] + z * cw[1:2, :] + dn * cw[2:3, :] + cb[...]
    v = u[:, :BR_W]
    x1 = u[:, BR_W:2 * BR_W]
    x2 = u[:, 2 * BR_W:]
    u2 = x2 * v
    hi, lo = _split2(u2)
    u_ref[:, :BR_W] = hi
    u_ref[:, BR_W:] = lo
    x1g_ref[...] = x1 * _silu(g_ref[...])


def _hy_pre(z, conv_w, conv_b, tps):
    rows = z.shape[0]
    return pl.pallas_call(
        functools.partial(_hy_pre_kernel, tps),
        out_shape=(jax.ShapeDtypeStruct((rows, 2 * BR_W), BF16),
                   jax.ShapeDtypeStruct((rows, BR_W), F32)),
        grid=(rows // ROW_TILE,),
        in_specs=_halo_specs(3 * BR_W, COL_HY_Z, rows) + [
            pl.BlockSpec((ROW_TILE, BR_W), lambda i: (i, COL_HY_G // BR_W)),
            pl.BlockSpec((3, 3 * BR_W), lambda i: (0, 0)),
            pl.BlockSpec((1, 3 * BR_W), lambda i: (0, 0))],
        out_specs=(pl.BlockSpec((ROW_TILE, 2 * BR_W), lambda i: (i, 0)),
                   pl.BlockSpec((ROW_TILE, BR_W), lambda i: (i, 0))),
        compiler_params=_params(("parallel",)),
        name="hy_pre",
    )(z, z, z, z, conv_w, conv_b.reshape(1, 3 * BR_W))


def _filter_kernel(L, zp, f1, fb1, fr, f2, fb2, f3, o_ref):
    freq = fr[...]
    hid = jnp.sin(freq * (_dot3(zp[...], f1[...]) + fb1[...]))
    hid = jnp.sin(freq * (_dot3(hid, f2[...]) + fb2[...]))
    h = _dot3(hid, f3[...])
    row = lax.broadcasted_iota(jnp.int32, (L, BR_W), 0)
    lane = lax.broadcasted_iota(jnp.int32, (L, BR_W), 1)
    t = row.astype(F32) / float(L - 1)
    delta = HY_MIN_DECAY + lane.astype(F32) * ((HY_MAX_DECAY - HY_MIN_DECAY) / float(BR_W - 1))
    win = jnp.exp(-t * jnp.abs(delta))
    hf = h[:, :BR_W] * win
    hb = jnp.where(row == 0, 0.0, h[:, BR_W:] * win)
    hfh, hfl = _split2(hf)
    hbh, hbl = _split2(hb)
    o_ref[:, 0 * BR_W:1 * BR_W] = hfh
    o_ref[:, 1 * BR_W:2 * BR_W] = hbh
    o_ref[:, 2 * BR_W:3 * BR_W] = hfl
    o_ref[:, 3 * BR_W:4 * BR_W] = hbl


def _hy_filters(L, zpos, f1p, fb1, freq, f2, fb2, f3):
    return pl.pallas_call(
        functools.partial(_filter_kernel, L),
        out_shape=jax.ShapeDtypeStruct((L, 4 * BR_W), BF16),
        compiler_params=_params(None),
        name="hy_filter",
    )(zpos, f1p, fb1.reshape(1, HY_FFN), freq.reshape(1, HY_FFN), f2, fb2.reshape(1, HY_FFN), f3)


def _filter_dft_kernel(tk, L, ch, cl, sh, sl, hfb, hre_ref, him_ref):
    j = pl.program_id(0)
    xh = hfb[:, :2 * BR_W]
    xl = hfb[:, 2 * BR_W:]
    fre = _dot3s(ch[...], cl[...], xh, xl)
    fim = _dot3s(sh[...], sl[...], xh, xl)
    row = lax.broadcasted_iota(jnp.int32, (tk, BR_W), 0) + j * tk
    is0 = row == 0
    scale = jnp.where(is0, 1.0 / (2 * L), 2.0 / (2 * L))
    hre_ref[...] = (fre[:, :BR_W] + fre[:, BR_W:]) * scale
    him_ref[...] = jnp.where(is0, fim[:, :BR_W] + fim[:, BR_W:], fim[:, :BR_W] - fim[:, BR_W:]) * scale


def _filter_dft(L, tk, tabs, hfb):
    ch, cl, sh, sl = tabs[:4]
    tab = pl.BlockSpec((tk, L), lambda j: (j, 0))
    return pl.pallas_call(
        functools.partial(_filter_dft_kernel, tk, L),
        out_shape=(jax.ShapeDtypeStruct((L, BR_W), F32), jax.ShapeDtypeStruct((L, BR_W), F32)),
        grid=(L // tk,),
        in_specs=[tab, tab, tab, tab, pl.BlockSpec((L, 4 * BR_W), lambda j: (0, 0))],
        out_specs=(pl.BlockSpec((tk, BR_W), lambda j: (j, 0)), pl.BlockSpec((tk, BR_W), lambda j: (j, 0))),
        compiler_params=_params(("parallel",)),
        name="hy_filter_dft",
    )(ch, cl, sh, sl, hfb)


def _dft_fwd_kernel(tk, ch, sh, u_ref, hre_ref, him_ref, y_ref):
    j = pl.program_id(1)
    uh = u_ref[:, :BR_W]
    ure = _dot(ch[...], uh)
    uim = _dot(sh[...], uh)
    hre = hre_ref[...]
    him = him_ref[...]
    row = lax.broadcasted_iota(jnp.int32, (tk, BR_W), 0) + j * tk
    is0 = row == 0
    y_ref[:, :BR_W] = jnp.where(is0, ure * hre, ure * hre - uim * him).astype(BF16)
    y_ref[:, BR_W:] = jnp.where(is0, uim * him, ure * him + uim * hre).astype(BF16)


def _dft_fwd(B, L, tk, tabs, u2, hre, him):
    ch, _, sh = tabs[:3]
    nk = L // tk
    tab = pl.BlockSpec((tk, L), lambda b, j: (j, 0))
    hsp = pl.BlockSpec((tk, BR_W), lambda b, j: (j, 0))
    return pl.pallas_call(
        functools.partial(_dft_fwd_kernel, tk),
        out_shape=jax.ShapeDtypeStruct((B * L, 2 * BR_W), BF16),
        grid=(B, nk),
        in_specs=[tab, tab, pl.BlockSpec((L, 2 * BR_W), lambda b, j: (b, 0)), hsp, hsp],
        out_specs=pl.BlockSpec((tk, 2 * BR_W), lambda b, j: (b * nk + j, 0)),
        compiler_params=_params(("parallel", "parallel")),
        name="hy_dft_fwd",
    )(ch, sh, u2, hre, him)


def _dft_inv_kernel(ch, sth, y_ref, u_ref, x1g_ref, bias_ref, o_ref):
    conv = _dot(ch[...], y_ref[:, :BR_W]) + _dot(sth[...], y_ref[:, BR_W:])
    u2 = u_ref[:, :BR_W].astype(F32) + u_ref[:, BR_W:].astype(F32)
    o_ref[...] = (conv + u2 * bias_ref[...]) * x1g_ref[...]


def _dft_inv(B, L, tm, tabs, y2, u2, x1g, bias):
    ch, sth = tabs[0], tabs[4]
    nt = L // tm
    tab = pl.BlockSpec((tm, L), lambda b, i: (i, 0))
    return pl.pallas_call(
        _dft_inv_kernel,
        out_shape=jax.ShapeDtypeStruct((B * L, BR_W), F32),
        grid=(B, nt),
        in_specs=[tab, tab,
                  pl.BlockSpec((L, 2 * BR_W), lambda b, i: (b, 0)),
                  pl.BlockSpec((tm, 2 * BR_W), lambda b, i: (b * nt + i, 0)),
                  pl.BlockSpec((tm, BR_W), lambda b, i: (b * nt + i, 0)),
                  pl.BlockSpec((1, BR_W), lambda b, i: (0, 0))],
        out_specs=pl.BlockSpec((tm, BR_W), lambda b, i: (b * nt + i, 0)),
        compiler_params=_params(("parallel", "parallel")),
        name="hy_dft_inv",
    )(ch, sth, y2, u2, x1g, bias.reshape(1, BR_W))


def _dft_tables(L):
    n = 2 * L
    k = jnp.arange(L, dtype=jnp.int32)
    m = (k[:, None] * k[None, :]) % n
    ang = m.astype(F32) * (2.0 * math.pi / n)
    c = jnp.cos(ang)
    s = -jnp.sin(ang)
    alt = jnp.where(k % 2 == 0, 1.0, -1.0).astype(F32)
    s = jnp.where((k == 0)[:, None], alt[None, :], s)
    out = []
    for t in (c, s):
        hi = t.astype(BF16)
        lo = (t - hi.astype(F32)).astype(BF16)
        out += [hi, lo]
    return tuple(out) + (out[2].T,)


def _hy_positions(L):
    t = jnp.linspace(0.0, 1.0, L, dtype=F32)[:, None]
    w = 2.0 * math.pi * jnp.arange(L, dtype=F32)[:, None] / L
    bands = jnp.linspace(1e-4, HY_BANDS - 1, HY_BANDS, dtype=F32)[None, :]
    z = jnp.concatenate([t, jnp.cos(bands * w), -jnp.sin(bands * w)], axis=-1)
    return jnp.pad(z, ((0, 0), (0, LANES - HY_EMB)))


def _rw_prep_kernel(tps, rc, rp, rn, kc, kp, kn, vc, vp, vn, lc, lp, ln,
                    mu_r, mu_k, mu_v, mu_l, kkw, ka, rk, w0, a0, w2, a2, ones_bd,
                    r_out, nkk_out, bonus_out, wf, bf, kdf, wb, bb, kdb, vt_out):
    i = pl.program_id(0)
    first = (i % tps) == 0
    last = (i % tps) == tps - 1

    def shifted(cur_ref, prev_ref, next_ref, mu_ref):
        cur = cur_ref[...]
        up, dn = _neighbours(cur, prev_ref[...], next_ref[...], first, last)
        return cur + mu_ref[...] * (0.5 * (up + dn) - cur)

    r = shifted(rc, rp, rn, mu_r)
    k = shifted(kc, kp, kn, mu_k)
    v = shifted(vc, vp, vn, mu_v)
    lora = shifted(lc, lp, ln, mu_l)
    ones = ones_bd[...]

    kk = k * kkw[...]
    kk = kk * lax.rsqrt(_segsum(kk * kk, ones) + 1e-12)
    r_out[...] = r
    nkk_out[...] = -kk
    for q in range(ROW_TILE // LANES):
        for p in range(RW_H // 2):
            vt = v[q * LANES:(q + 1) * LANES, p * LANES:(p + 1) * LANES].T
            for s in range(LANES // SUB):
                vt_out[q * (LANES // SUB) + s, p] = jnp.concatenate(
                    [vt[:RW_N, s * SUB:(s + 1) * SUB], vt[RW_N:, s * SUB:(s + 1) * SUB]], axis=1)

    kd_sum = None
    outs = ((wf, bf, kdf), (wb, bb, kdb))
    for d in range(2):
        wl = lora[:, d * W_LORA:(d + 1) * W_LORA]
        al = lora[:, 2 * W_LORA + d * A_LORA:2 * W_LORA + (d + 1) * A_LORA]
        w = jnp.exp(-RW_DECAY_SCALE * _sigmoid(w0[d:d + 1, :] + _dot3(jnp.tanh(wl), w2[d])))
        a = _sigmoid(a0[d:d + 1, :] + _dot3(al, a2[d]))
        kd = k * (1.0 + (a - 1.0) * ka[...])
        outs[d][0][...] = w
        outs[d][1][...] = kk * a
        outs[d][2][...] = kd
        kd_sum = kd if kd_sum is None else kd_sum + kd
    bonus_out[...] = _segsum(r * (0.5 * kd_sum) * rk[...], ones) * v


def _rw_prep(z, p, ones_bd, tps):
    rows = z.shape[0]
    mu = p["rw_mu"]
    vec = lambda a: a.reshape(1, -1)
    full = lambda a: pl.BlockSpec(a.shape, lambda i: (0,) * a.ndim)
    smalls = [vec(mu[0:512]), vec(mu[512:1024]), vec(mu[1024:1536]), vec(mu[1536:1792]),
              vec(p["rw_kk"]), vec(p["rw_ka"]), vec(p["rw_rk"]), p["rw_w0"], p["rw_a0"],
              p["rw_w2"], p["rw_a2"], ones_bd]
    tile = pl.BlockSpec((ROW_TILE, BR_W), lambda i: (i, 0))
    npair = RW_H // 2
    vt_shape = jax.ShapeDtypeStruct((rows // SUB, npair, RW_N, LANES), F32)
    vt_spec = pl.BlockSpec((ROW_TILE // SUB, npair, RW_N, LANES), lambda i: (i, 0, 0, 0))
    return pl.pallas_call(
        functools.partial(_rw_prep_kernel, tps),
        out_shape=tuple(jax.ShapeDtypeStruct((rows, BR_W), F32) for _ in range(9)) + (vt_shape,),
        grid=(rows // ROW_TILE,),
        in_specs=(_halo_specs(BR_W, COL_RW_R, rows) + _halo_specs(BR_W, COL_RW_K, rows)
                  + _halo_specs(BR_W, COL_RW_V, rows) + _halo_specs(2 * LANES, COL_LORA, rows)
                  + [full(a) for a in smalls]),
        out_specs=tuple(tile for _ in range(9)) + (vt_spec,),
        compiler_params=_params(("parallel",)),
        name="rw_prep",
    )(*([z] * 12), *smalls)


def _scan_kernel(nc, bb, r_f, nk_f, w_f, b_f, kd_f, vt_f, r_b, nk_b, w_b, b_b, kd_b, vt_b,
                 s0, ones2, yt_f, yt_b, sfin, state):
    c = pl.program_id(1)
    npair = RW_H // 2
    nsub = SCAN_CHUNK // SUB
    chains = [(bi, d, p) for bi in range(bb) for d in range(2) for p in range(npair)]

    @pl.when(c == 0)
    def _():
        for ci, (bi, d, p) in enumerate(chains):
            state[ci] = s0[bi, d, p]

    g4 = ones2[...]
    lane_full = lax.broadcasted_iota(jnp.int32, (RW_N, LANES), 1)
    lane = lane_full & (SUB - 1)
    lane_head = lane_full & SUB
    dirs = ((r_f, nk_f, w_f, b_f, kd_f, vt_f, yt_f), (r_b, nk_b, w_b, b_b, kd_b, vt_b, yt_b))

    for sub in range(nsub):
        subs = (sub, nsub - 1 - sub)
        for bi, d, p in chains:
            dirs[d][6][bi, subs[d], p] = jnp.zeros((RW_N, LANES), F32)

        def body(g, carry):
            bases = (pl.multiple_of(g * HALO, HALO), pl.multiple_of((SUB // HALO - 1 - g) * HALO, HALO))
            for jj in range(HALO):
                js = (jj, HALO - 1 - jj)

                def row(ref, bi, d, p):
                    blk = ref[bi, pl.ds(subs[d] * SUB + bases[d], HALO), pl.ds(p * LANES, LANES)]
                    return blk[js[d]:js[d] + 1, :]

                zs = [(state[ci] * row(dirs[d][1], bi, d, p)).astype(BF16) for ci, (bi, d, p) in enumerate(chains)]
                lhs = jnp.concatenate([jnp.concatenate(zs[i:i + 2], axis=1) for i in range(0, len(zs), 2)], axis=0)
                res = _dot(lhs, g4)
                zr = []
                for ci, (bi, d, p) in enumerate(chains):
                    tt = bases[d] + js[d]
                    vb = jnp.take_along_axis(dirs[d][5][bi, subs[d], p], lane_head + tt, axis=1)
                    sa = res[(ci // 2) * RW_N:(ci // 2 + 1) * RW_N, (ci % 2) * LANES:(ci % 2 + 1) * LANES]
                    st = (state[ci] * row(dirs[d][2], bi, d, p) + sa * row(dirs[d][3], bi, d, p)
                          + vb * row(dirs[d][4], bi, d, p))
                    state[ci] = st
                    zr.append((st * row(dirs[d][0], bi, d, p)).astype(BF16))
                lhs = jnp.concatenate([jnp.concatenate(zr[i:i + 2], axis=1) for i in range(0, len(zr), 2)], axis=0)
                res = _dot(lhs, g4)
                for ci, (bi, d, p) in enumerate(chains):
                    y_ref = dirs[d][6]
                    sel = lane == bases[d] + js[d]
                    yb = res[(ci // 2) * RW_N:(ci // 2 + 1) * RW_N, (ci % 2) * LANES:(ci % 2 + 1) * LANES]
                    y_ref[bi, subs[d], p] = jnp.where(sel, yb, y_ref[bi, subs[d], p])
            return carry

        lax.fori_loop(0, SUB // HALO, body, 0)

    @pl.when(c == nc - 1)
    def _():
        for ci, (bi, d, p) in enumerate(chains):
            sfin[bi, d, p] = state[ci]


def _rw_scan(B, L, seqs, vt, s0, ones2):
    r, nkk, wf, bf, kdf, wb, bwb, kdb = [a.reshape(B, L, BR_W) for a in seqs]
    bb = SCAN_BATCH
    nc = L // SCAN_CHUNK
    nsub = SCAN_CHUNK // SUB
    npair = RW_H // 2
    nch = bb * 2 * npair
    fwd = pl.BlockSpec((bb, SCAN_CHUNK, BR_W), lambda b, c: (b, c, 0))
    bwd = pl.BlockSpec((bb, SCAN_CHUNK, BR_W), lambda b, c: (b, nc - 1 - c, 0))
    tile = (bb, nsub, npair, RW_N, LANES)
    vfwd = pl.BlockSpec(tile, lambda b, c: (b, c, 0, 0, 0))
    vbwd = pl.BlockSpec(tile, lambda b, c: (b, nc - 1 - c, 0, 0, 0))
    sspec = pl.BlockSpec((bb, 2, npair, RW_N, LANES), lambda b, c: (b, 0, 0, 0, 0))
    yshape = jax.ShapeDtypeStruct((B, L // SUB, npair, RW_N, LANES), F32)
    return pl.pallas_call(
        functools.partial(_scan_kernel, nc, bb),
        out_shape=(yshape, yshape, jax.ShapeDtypeStruct((B, 2, npair, RW_N, LANES), F32)),
        grid=(B // bb, nc),
        in_specs=[fwd, fwd, fwd, fwd, fwd, vfwd, bwd, bwd, bwd, bwd, bwd, vbwd, sspec,
                  pl.BlockSpec((2 * LANES, 2 * LANES), lambda b, c: (0, 0))],
        out_specs=(vfwd, vbwd, sspec),
        scratch_shapes=[pltpu.VMEM((nch, RW_N, LANES), F32)],
        compiler_params=_params(("parallel", "arbitrary")),
        name="rw_scan",
    )(r, nkk, wf, bf, kdf, vt, r, nkk, wb, bwb, kdb, vt, s0, ones2)


def _rw_post_kernel(yf, yb, bonus, g_ref, lnw, lnb, ones_bd, o_ref, y_sc):
    ones = ones_bd[...]
    nsub = LANES // SUB
    for q in range(ROW_TILE // LANES):
        for p in range(RW_H // 2):
            tiles = [yf[q * nsub + s, p] + yb[q * nsub + s, p] for s in range(nsub)]
            heads = [jnp.concatenate([t[:, h * SUB:(h + 1) * SUB] for t in tiles], axis=1) for h in range(2)]
            y_sc[q * LANES:(q + 1) * LANES, p * LANES:(p + 1) * LANES] = jnp.concatenate(heads, axis=0).T
    y = y_sc[...]
    mu = _segsum(y, ones) * (1.0 / RW_N)
    yc = y - mu
    var = _segsum(yc * yc, ones) * (1.0 / RW_N)
    y = yc * lax.rsqrt(var + RW_GN_EPS) * lnw[...] + lnb[...]
    o_ref[...] = (y + bonus[...]) * _silu(g_ref[...])


def _rw_post(ytf, ytb, bonus, z, ln_w, ln_b, ones_bd):
    rows = bonus.shape[0]
    npair = RW_H // 2
    tile = pl.BlockSpec((ROW_TILE, BR_W), lambda i: (i, 0))
    ytile = pl.BlockSpec((ROW_TILE // SUB, npair, RW_N, LANES), lambda i: (i, 0, 0, 0))
    vec = pl.BlockSpec((1, BR_W), lambda i: (0, 0))
    return pl.pallas_call(
        _rw_post_kernel,
        out_shape=jax.ShapeDtypeStruct((rows, BR_W), F32),
        grid=(rows // ROW_TILE,),
        in_specs=[ytile, ytile, tile, pl.BlockSpec((ROW_TILE, BR_W), lambda i: (i, COL_RW_G // BR_W)),
                  vec, vec, pl.BlockSpec((BR_W, BR_W), lambda i: (0, 0))],
        out_specs=tile,
        scratch_shapes=[pltpu.VMEM((ROW_TILE, BR_W), F32)],
        compiler_params=_params(("parallel",)),
        name="rw_post",
    )(ytf, ytb, bonus, z, ln_w.reshape(1, BR_W), ln_b.reshape(1, BR_W), ones_bd)


def _rope(x, cos, sin_signed):
    lane = lax.broadcasted_iota(jnp.int32, x.shape, 1)
    low = (lane & (ROPE_AXIS - 1)) < (ROPE_AXIS // 2)
    width = x.shape[1]
    partner = jnp.where(low, pltpu.roll(x, width - ROPE_AXIS // 2, 1), pltpu.roll(x, ROPE_AXIS // 2, 1))
    return x * cos + partner * sin_signed


def _da_prep_kernel(rope, q_ref, k_ref, gq, gk, ones_bd, *rest):
    if rope:
        cos_ref, sin_ref, q_out, k_out, kn_out = rest
    else:
        q_out, k_out, kn_out = rest
    ones = ones_bd[...]

    def norm(x, g):
        ms = _segsum(x * x, ones) * (1.0 / DA_DH)
        return x * lax.rsqrt(ms + NORM_EPS) * g

    qn = norm(q_ref[...], gq[...])
    kn = norm(k_ref[...], gk[...])
    kn_out[...] = kn
    if rope:
        qn = _rope(qn, cos_ref[...], sin_ref[...])
        kn = _rope(kn, cos_ref[...], sin_ref[...])
    k_out[...] = kn.astype(BF16)
    qn = qn * (DA_DH ** -0.5)
    lane = lax.broadcasted_iota(jnp.int32, (qn.shape[0], LANES), 1)
    for h in range(DA_H):
        qh = qn[:, h * LANES:(h + 1) * LANES]
        q_out[:, (2 * h) * LANES:(2 * h + 1) * LANES] = jnp.where(lane < DA_DH, qh, 0.0).astype(BF16)
        q_out[:, (2 * h + 1) * LANES:(2 * h + 2) * LANES] = jnp.where(lane >= DA_DH, qh, 0.0).astype(BF16)


def _da_prep(z, gq, gk, ones_bd, tps, rope_tabs):
    rows = z.shape[0]
    rope = rope_tabs is not None
    tile = lambda col: pl.BlockSpec((ROW_TILE, BR_W), lambda i: (i, col // BR_W))
    vec = pl.BlockSpec((1, BR_W), lambda i: (0, 0))
    in_specs = [tile(COL_DA_Q), tile(COL_DA_K), vec, vec, pl.BlockSpec((BR_W, BR_W), lambda i: (0, 0))]
    args = [z, z, jnp.tile(gq, 2 * DA_H).reshape(1, BR_W), jnp.tile(gk, 2 * DA_H).reshape(1, BR_W), ones_bd]
    if rope:
        tab = pl.BlockSpec((ROW_TILE, BR_W), lambda i: (i % tps, 0))
        in_specs += [tab, tab]
        args += list(rope_tabs)
    return pl.pallas_call(
        functools.partial(_da_prep_kernel, rope),
        out_shape=(jax.ShapeDtypeStruct((rows, 2 * BR_W), BF16),
                   jax.ShapeDtypeStruct((rows, BR_W), BF16),
                   jax.ShapeDtypeStruct((rows, BR_W), F32)),
        grid=(rows // ROW_TILE,),
        in_specs=in_specs,
        out_specs=(pl.BlockSpec((ROW_TILE, 2 * BR_W), lambda i: (i, 0)),
                   pl.BlockSpec((ROW_TILE, BR_W), lambda i: (i, 0)),
                   pl.BlockSpec((ROW_TILE, BR_W), lambda i: (i, 0))),
        compiler_params=_params(("parallel",)),
        name="da_prep",
    )(*args)


def _rope_tables(L):
    half = ROPE_AXIS // 2
    lane = jnp.arange(BR_W)
    j = lane % DA_DH
    use_col = (j // ROPE_AXIS) == 1
    idx = (j % ROPE_AXIS) % half
    freqs = ROPE_BASE ** (-idx.astype(F32) / half)
    t = jnp.arange(L)
    pos = jnp.where(use_col[None, :], (t % GRID_W)[:, None], (t // GRID_W)[:, None]).astype(F32)
    ang = pos * freqs[None, :]
    sign = jnp.where((j % ROPE_AXIS) < half, -1.0, 1.0).astype(F32)
    return jnp.cos(ang), jnp.sin(ang) * sign[None, :]


def _da_attn_kernel(lam_init, q_ref, k_ref, v_ref, g_ref, gsub, lq1, lk1, lq2, lk2, o_ref):
    lam = (jnp.exp(jnp.sum(lq1[...] * lk1[...], axis=-1, keepdims=True))
           - jnp.exp(jnp.sum(lq2[...] * lk2[...], axis=-1, keepdims=True)) + lam_init)
    nt = (((1,), (1,)), ((), ()))
    for h in range(DA_H):
        kh = k_ref[0, :, h * LANES:(h + 1) * LANES]
        vh = v_ref[0, :, h * LANES:(h + 1) * LANES]
        outs = []
        for m in range(2):
            q = q_ref[:, (2 * h + m) * LANES:(2 * h + m + 1) * LANES]
            s = lax.dot_general(q, kh, nt, preferred_element_type=F32)
            e = jnp.exp(s - jnp.max(s, axis=-1, keepdims=True))
            den = jnp.sum(e, axis=-1, keepdims=True)
            outs.append(_dot(e.astype(BF16), vh) / den)
        o = outs[0] - lam * outs[1]
        o = o * lax.rsqrt(jnp.mean(o * o, axis=-1, keepdims=True) + 1e-5) * gsub[...] * (1.0 - lam_init)
        o_ref[:, h * LANES:(h + 1) * LANES] = o * _silu(g_ref[:, h * LANES:(h + 1) * LANES])


def _da_attn(B, L, lam_init, q2, kcat, vcat, z, p):
    lk = kcat.shape[1]
    nt = L // ROW_TILE
    vec = lambda a: a.reshape(1, -1)
    small = lambda n: pl.BlockSpec((1, n), lambda b, i: (0, 0))
    return pl.pallas_call(
        functools.partial(_da_attn_kernel, lam_init),
        out_shape=jax.ShapeDtypeStruct((B * L, BR_W), F32),
        grid=(B, nt),
        in_specs=[pl.BlockSpec((ROW_TILE, 2 * BR_W), lambda b, i: (b * nt + i, 0)),
                  pl.BlockSpec((1, lk, BR_W), lambda b, i: (b, 0, 0)),
                  pl.BlockSpec((1, lk, BR_W), lambda b, i: (b, 0, 0)),
                  pl.BlockSpec((ROW_TILE, BR_W), lambda b, i: (b * nt + i, COL_DA_G // BR_W)),
                  small(2 * DA_DH), small(DA_DH), small(DA_DH), small(DA_DH), small(DA_DH)],
        out_specs=pl.BlockSpec((ROW_TILE, BR_W), lambda b, i: (b * nt + i, 0)),
        compiler_params=_params(("parallel", "parallel")),
        name="da_attn",
    )(q2, kcat, vcat, z, vec(p["da_gsub"]), vec(p["da_lq1"]), vec(p["da_lk1"]),
      vec(p["da_lq2"]), vec(p["da_lk2"]))


def _merge_kernel(ya, yb, yc, mga, mgb, mgc, x_ref, mod_ref, wbr, wout, o_ref):
    merged = (_sigmoid(mga[...]) * _dot(ya[...].astype(BF16), wbr[0])
              + _sigmoid(mgb[...]) * _dot(yb[...].astype(BF16), wbr[1])
              + _sigmoid(mgc[...]) * _dot(yc[...].astype(BF16), wbr[2]))
    out = _dot(merged.astype(BF16), wout[...])
    o_ref[...] = x_ref[...] + mod_ref[0][2:3] * out


def _merge(ya, yb, yc, z, x, mod, wbr_bf, wout_bf, tps):
    rows = x.shape[0]
    tile = pl.BlockSpec((ROW_TILE, BR_W), lambda i: (i, 0))
    mg = lambda n: pl.BlockSpec((ROW_TILE, D_MODEL), lambda i: (i, COL_MG // D_MODEL + n))
    return pl.pallas_call(
        _merge_kernel,
        out_shape=jax.ShapeDtypeStruct((rows, D_MODEL), F32),
        grid=(rows // ROW_TILE,),
        in_specs=[tile, tile, tile, mg(0), mg(1), mg(2),
                  pl.BlockSpec((ROW_TILE, D_MODEL), lambda i: (i, 0)),
                  pl.BlockSpec((1, 3, D_MODEL), lambda i: (i // tps, 0, 0)),
                  pl.BlockSpec((3, BR_W, D_MODEL), lambda i: (0, 0, 0)),
                  pl.BlockSpec((D_MODEL, D_MODEL), lambda i: (0, 0))],
        out_specs=pl.BlockSpec((ROW_TILE, D_MODEL), lambda i: (i, 0)),
        compiler_params=_params(("parallel",)),
        name="merge_out",
    )(ya, yb, yc, z, z, z, x, mod, wbr_bf, wout_bf)


def _state_to_pairs(s):
    B = s.shape[0]
    t = s.reshape(B, 2, RW_H // 2, 2, RW_N, RW_N)
    return t.transpose(0, 1, 2, 4, 3, 5).reshape(B, 2, RW_H // 2, RW_N, 2 * RW_N)


def _state_from_pairs(s):
    B = s.shape[0]
    t = s.reshape(B, 2, RW_H // 2, RW_N, 2, RW_N)
    return t.transpose(0, 1, 2, 4, 3, 5).reshape(B, 2, RW_H, RW_N, RW_N)


def _permute_w_in(w):
    hy_z, hy_g = w[:, 0:1536], w[:, 1536:2048]
    rw_rkv, rw_lora, rw_g = w[:, 2048:3584], w[:, 3584:3840], w[:, 3840:4352]
    da = w[:, 4352:6400]
    mg = w[:, 6400:9472]
    return jnp.concatenate([hy_z, hy_g, rw_rkv, rw_g, da, mg, rw_lora], axis=1)


def _layer(x, mod, p, consts, B, L, lam_init, ctx):
    tps = L // ROW_TILE
    tabs, zpos, rope_tabs, ones_bd, ones2 = consts
    z = _inproj(x, mod, p["norm_g"], p["w_in_bf"], tps)

    u2, x1g = _hy_pre(z, p["hy_conv_w"], p["hy_conv_b"], tps)
    tk = min(L, 256)
    hfb = _hy_filters(L, zpos, p["hy_f1p"], p["hy_fb1"], p["hy_freq"], p["hy_f2"], p["hy_fb2"], p["hy_f3"])
    hre, him = _filter_dft(L, tk, tabs, hfb)
    y4 = _dft_fwd(B, L, tk, tabs, u2, hre, him)
    y_a = _dft_inv(B, L, min(L, 512), tabs, y4, u2, x1g, p["hy_bias"])

    r, nkk, bonus, wf, bf, kdf, wb, bb, kdb, vt = _rw_prep(z, p, ones_bd, tps)
    npair = RW_H // 2
    if ctx is None:
        s0 = jnp.zeros((B, 2, npair, RW_N, LANES), F32)
    else:
        s0 = _state_to_pairs(ctx[2])
    tiles = (B, L // SUB, npair, RW_N, LANES)
    ytf, ytb, sfin = _rw_scan(B, L, (r, nkk, wf, bf, kdf, wb, bb, kdb), vt.reshape(tiles), s0, ones2)
    flat = (B * L // SUB, npair, RW_N, LANES)
    y_b = _rw_post(ytf.reshape(flat), ytb.reshape(flat), bonus, z, p["rw_ln_w"], p["rw_ln_b"], ones_bd)

    q2, k_bf, kn = _da_prep(z, p["da_gq"], p["da_gk"], ones_bd, tps, rope_tabs if ctx is not None else None)
    v_da = z[:, COL_DA_V:COL_DA_V + BR_W]
    k3 = k_bf.reshape(B, L, BR_W)
    v3 = v_da.astype(BF16).reshape(B, L, BR_W)
    if ctx is not None:
        k3 = jnp.concatenate([ctx[0].reshape(B, -1, BR_W).astype(BF16), k3], axis=1)
        v3 = jnp.concatenate([ctx[1].reshape(B, -1, BR_W).astype(BF16), v3], axis=1)
    y_c = _da_attn(B, L, lam_init, q2, k3, v3, z, p)

    x_new = _merge(y_a, y_b, y_c, z, x, mod, p["w_br_bf"], p["w_out_bf"], tps)
    return x_new, (kn, v_da, sfin)


def kernel(x_prompt, x_sample, cache_k, cache_v, state_rwkv, c, c_ctx, norm_g, w_ada, b_ada, w_in, hy_conv_w, hy_conv_b, hy_f1, hy_fb1, hy_freq, hy_f2, hy_fb2, hy_f3, hy_bias, rw_mu, rw_w0, rw_w2, rw_a0, rw_a2, rw_kk, rw_ka, rw_rk, rw_ln_w, rw_ln_b, da_gq, da_gk, da_lq1, da_lk1, da_lq2, da_lk2, da_gsub, w_br, w_out):
    Bp, Lp, _ = x_prompt.shape
    Bs, Ls, _ = x_sample.shape

    pad = (-(Bp + Bs)) % 8
    c_all = jnp.concatenate([jnp.broadcast_to(c_ctx, (Bp, D_MODEL)), c, jnp.zeros((pad, D_MODEL), F32)], axis=0)
    mod_all = _modulation(c_all, w_ada, b_ada)
    mod_p = mod_all[:, :Bp].reshape(DEPTH, Bp, 3, D_MODEL)
    mod_s = mod_all[:, Bp:Bp + Bs].reshape(DEPTH, Bs, 3, D_MODEL)

    seg = jnp.arange(BR_W) // RW_N
    ones_bd = (seg[:, None] == seg[None, :]).astype(BF16)
    ones2 = ones_bd[:2 * LANES, :2 * LANES]

    consts = {}
    for L in sorted({Lp, Ls}):
        consts[L] = (_dft_tables(L), _hy_positions(L), _rope_tables(L), ones_bd, ones2)

    def layer_params(l):
        return dict(norm_g=norm_g[l], w_in_bf=_permute_w_in(w_in[l]).astype(BF16),
                    hy_conv_w=hy_conv_w[l], hy_conv_b=hy_conv_b[l],
                    hy_f1p=jnp.pad(hy_f1[l], ((0, LANES - HY_EMB), (0, 0))), hy_fb1=hy_fb1[l],
                    hy_freq=hy_freq[l], hy_f2=hy_f2[l], hy_fb2=hy_fb2[l], hy_f3=hy_f3[l],
                    hy_bias=hy_bias[l], rw_mu=rw_mu[l], rw_w0=rw_w0[l], rw_w2=rw_w2[l],
                    rw_a0=rw_a0[l], rw_a2=rw_a2[l], rw_kk=rw_kk[l], rw_ka=rw_ka[l], rw_rk=rw_rk[l],
                    rw_ln_w=rw_ln_w[l], rw_ln_b=rw_ln_b[l], da_gq=da_gq[l], da_gk=da_gk[l],
                    da_lq1=da_lq1[l], da_lk1=da_lk1[l], da_lq2=da_lq2[l], da_lk2=da_lk2[l],
                    da_gsub=da_gsub[l], w_br_bf=w_br[l].astype(BF16), w_out_bf=w_out[l].astype(BF16))

    params = [layer_params(l) for l in range(DEPTH)]

    xp = x_prompt.reshape(Bp * Lp, D_MODEL)
    new_k, new_v, new_s = [], [], []
    for l in range(DEPTH):
        lam_init = 0.8 - 0.6 * math.exp(-0.3 * l)
        xp, (k_l, v_l, s_l) = _layer(xp, mod_p[l], params[l], consts[Lp], Bp, Lp, lam_init, None)
        new_k.append(k_l.reshape(Bp, Lp, DA_H, 2, DA_DH))
        new_v.append(v_l.reshape(Bp, Lp, DA_H, 2 * DA_DH))
        new_s.append(_state_from_pairs(s_l))
    new_cache_k = jnp.stack(new_k, axis=1)
    new_cache_v = jnp.stack(new_v, axis=1)
    new_state = jnp.stack(new_s, axis=1)

    xs = x_sample.reshape(Bs * Ls, D_MODEL)
    for l in range(DEPTH):
        lam_init = 0.8 - 0.6 * math.exp(-0.3 * l)
        ctx = (cache_k[:, l], cache_v[:, l], state_rwkv[:, l])
        xs, _ = _layer(xs, mod_s[l], params[l], consts[Ls], Bs, Ls, lam_init, ctx)

    return (xp.reshape(Bp, Lp, D_MODEL), xs.reshape(Bs, Ls, D_MODEL), new_cache_k, new_cache_v, new_state)
```

```python
import functools
import math

import jax
import jax.numpy as jnp
from jax import lax
from jax.experimental import pallas as pl
from jax.experimental.pallas import tpu as pltpu

F32 = jnp.float32
BF16 = jnp.bfloat16

D_MODEL = 1024
DEPTH = 4
GRID_W = 64
BR_W = 512
HY_EMB = 33
HY_BANDS = (HY_EMB - 1) // 2
HY_FFN = 64
HY_MAX_DECAY = math.log(1e-2) / 0.3
HY_MIN_DECAY = math.log(1e-2) / 1.5
RW_N = 64
RW_H = BR_W // RW_N
W_LORA = 64
A_LORA = 64
RW_DECAY_SCALE = math.exp(-0.5)
RW_GN_EPS = 64e-5
DA_DH = 64
DA_H = BR_W // (2 * DA_DH)
ROPE_AXIS = DA_DH // 2
ROPE_BASE = 10000.0
NORM_EPS = 1e-6
N_IN = 9472

LANES = 128
ROW_TILE = 256
HALO = 8
SCAN_CHUNK = 128
SCAN_BATCH = 2
SCAN_GROUP = 16
SUB = 64
VMEM_LIMIT = 52 * 1024 * 1024

COL_HY_Z = 0
COL_HY_G = 1536
COL_RW_R = 2048
COL_RW_K = 2560
COL_RW_V = 3072
COL_RW_G = 3584
COL_DA_Q = 4096
COL_DA_K = 4608
COL_DA_V = 5120
COL_DA_G = 5632
COL_MG = 6144
COL_LORA = 9216


def _sigmoid(x):
    return 1.0 / (1.0 + jnp.exp(-x))


def _silu(x):
    return x * _sigmoid(x)


def _dot(a, b):
    return jnp.dot(a, b, preferred_element_type=F32)


def _split2(x):
    hi = x.astype(BF16)
    lo = (x - hi.astype(F32)).astype(BF16)
    return hi, lo


def _split3(x):
    hi = x.astype(BF16)
    r1 = x - hi.astype(F32)
    mid = r1.astype(BF16)
    lo = (r1 - mid.astype(F32)).astype(BF16)
    return hi, mid, lo


def _dot3(a, b):
    ah, al = _split2(a)
    bh, bl = _split2(b)
    return _dot(ah, bh) + _dot(ah, bl) + _dot(al, bh)


def _segsum(x, ones_bd):
    hi, mid, lo = _split3(x)
    return _dot(hi, ones_bd) + _dot(mid, ones_bd) + _dot(lo, ones_bd)


def _params(sem, vmem=VMEM_LIMIT):
    return pltpu.CompilerParams(dimension_semantics=sem, vmem_limit_bytes=vmem)


def _mod_kernel(c_ref, w_ref, b_ref, o_ref):
    c = c_ref[...]
    o_ref[0] = _dot3(_silu(c), w_ref[0]) + b_ref[0]


def _modulation(c_all, w_ada, b_ada):
    rows = c_all.shape[0]
    nt = 3 * D_MODEL // 1024
    return pl.pallas_call(
        _mod_kernel,
        out_shape=jax.ShapeDtypeStruct((DEPTH, rows, 3 * D_MODEL), F32),
        grid=(DEPTH, nt),
        in_specs=[pl.BlockSpec((rows, D_MODEL), lambda l, j: (0, 0)),
                  pl.BlockSpec((1, D_MODEL, 1024), lambda l, j: (l, 0, j)),
                  pl.BlockSpec((1, 1, 1024), lambda l, j: (l, 0, j))],
        out_specs=pl.BlockSpec((1, rows, 1024), lambda l, j: (l, 0, j)),
        compiler_params=_params(("parallel", "parallel")),
        name="adaln_mod",
    )(c_all, w_ada, b_ada.reshape(DEPTH, 1, 3 * D_MODEL))


def _inproj_kernel(x_ref, mod_ref, g_ref, w_ref, o_ref):
    x = x_ref[...]
    ms = jnp.mean(x * x, axis=-1, keepdims=True)
    y = x * lax.rsqrt(ms + NORM_EPS) * g_ref[...]
    mod = mod_ref[0]
    h = y * (1.0 + mod[1:2]) + mod[0:1]
    o_ref[...] = _dot(h.astype(BF16), w_ref[...])


def _inproj(x, mod, norm_g, w_in_bf, tps):
    rows = x.shape[0]
    half = N_IN // 2
    return pl.pallas_call(
        _inproj_kernel,
        out_shape=jax.ShapeDtypeStruct((rows, N_IN), F32),
        grid=(2, rows // ROW_TILE),
        in_specs=[pl.BlockSpec((ROW_TILE, D_MODEL), lambda j, i: (i, 0)),
                  pl.BlockSpec((1, 3, D_MODEL), lambda j, i: (i // tps, 0, 0)),
                  pl.BlockSpec((1, D_MODEL), lambda j, i: (0, 0)),
                  pl.BlockSpec((D_MODEL, half), lambda j, i: (0, j))],
        out_specs=pl.BlockSpec((ROW_TILE, half), lambda j, i: (i, j)),
        compiler_params=_params(("parallel", "parallel")),
        name="inproj",
    )(x, mod, norm_g.reshape(1, D_MODEL), w_in_bf)


def _neighbours(cur, prev_blk, next_blk, first, last):
    rows = cur.shape[0]
    rid = lax.broadcasted_iota(jnp.int32, cur.shape, 0)
    prow = jnp.where(first, 0.0, prev_blk[HALO - 1:HALO, :])
    nrow = jnp.where(last, 0.0, next_blk[0:1, :])
    up = jnp.where(rid == 0, prow, pltpu.roll(cur, 1, 0))
    dn = jnp.where(rid == rows - 1, nrow, pltpu.roll(cur, rows - 1, 0))
    return up, dn


def _halo_specs(width, col, rows):
    per = ROW_TILE // HALO
    nblk = rows // HALO
    cb = col // width
    return [pl.BlockSpec((ROW_TILE, width), lambda i: (i, cb)),
            pl.BlockSpec((HALO, width), lambda i: (jnp.maximum(i * per - 1, 0), cb)),
            pl.BlockSpec((HALO, width), lambda i: (jnp.minimum((i + 1) * per, nblk - 1), cb))]


def _hy_pre_kernel(tps, zc, zp, zn, g_ref, cw, cb, u_ref, x1g_ref):
    i = pl.program_id(0)
    first = (i % tps) == 0
    last = (i % tps) == tps - 1
    z = zc[...]
    up, dn = _neighbours(z, zp[...], zn[...], first, last)
    u = up * cw[0:1, :] + z * cw[1:2, :] + dn * cw[2:3, :] + cb[...]
    v = u[:, :BR_W]
    x1 = u[:, BR_W:2 * BR_W]
    x2 = u[:, 2 * BR_W:]
    u2 = x2 * v
    hi, lo = _split2(u2)
    u_ref[:, :BR_W] = hi
    u_ref[:, BR_W:] = lo
    x1g_ref[...] = x1 * _silu(g_ref[...])


def _hy_pre(z, conv_w, conv_b, tps):
    rows = z.shape[0]
    return pl.pallas_call(
        functools.partial(_hy_pre_kernel, tps),
        out_shape=(jax.ShapeDtypeStruct((rows, 2 * BR_W), BF16),
                   jax.ShapeDtypeStruct((rows, BR_W), F32)),
        grid=(rows // ROW_TILE,),
        in_specs=_halo_specs(3 * BR_W, COL_HY_Z, rows) + [
            pl.BlockSpec((ROW_TILE, BR_W), lambda i: (i, COL_HY_G // BR_W)),
            pl.BlockSpec((3, 3 * BR_W), lambda i: (0, 0)),
            pl.BlockSpec((1, 3 * BR_W), lambda i: (0, 0))],
        out_specs=(pl.BlockSpec((ROW_TILE, 2 * BR_W), lambda i: (i, 0)),
                   pl.BlockSpec((ROW_TILE, BR_W), lambda i: (i, 0))),
        compiler_params=_params(("parallel",)),
        name="hy_pre",
    )(z, z, z, z, conv_w, conv_b.reshape(1, 3 * BR_W))


def _filter_kernel(L, zp, f1, fb1, fr, f2, fb2, f3, o_ref):
    freq = fr[...]
    hid = jnp.sin(freq * (_dot3(zp[...], f1[...]) + fb1[...]))
    hid = jnp.sin(freq * (_dot3(hid, f2[...]) + fb2[...]))
    h = _dot3(hid, f3[...])
    row = lax.broadcasted_iota(jnp.int32, (L, BR_W), 0)
    lane = lax.broadcasted_iota(jnp.int32, (L, BR_W), 1)
    t = row.astype(F32) / float(L - 1)
    delta = HY_MIN_DECAY + lane.astype(F32) * ((HY_MAX_DECAY - HY_MIN_DECAY) / float(BR_W - 1))
    win = jnp.exp(-t * jnp.abs(delta))
    hf = h[:, :BR_W] * win
    hb = jnp.where(row == 0, 0.0, h[:, BR_W:] * win)
    o_ref[:, :BR_W] = hf.astype(BF16)
    o_ref[:, BR_W:] = hb.astype(BF16)


def _hy_filters(L, zpos, f1p, fb1, freq, f2, fb2, f3):
    return pl.pallas_call(
        functools.partial(_filter_kernel, L),
        out_shape=jax.ShapeDtypeStruct((L, 2 * BR_W), BF16),
        compiler_params=_params(None),
        name="hy_filter",
    )(zpos, f1p, fb1.reshape(1, HY_FFN), freq.reshape(1, HY_FFN), f2, fb2.reshape(1, HY_FFN), f3)


def _filter_dft_kernel(tk, L, ch, sh, hfb, hre_ref, him_ref):
    j = pl.program_id(0)
    fre = _dot(ch[...], hfb[...])
    fim = _dot(sh[...], hfb[...])
    row = lax.broadcasted_iota(jnp.int32, (tk, BR_W), 0) + j * tk
    is0 = row == 0
    scale = jnp.where(is0, 1.0 / (2 * L), 2.0 / (2 * L))
    hre_ref[...] = (fre[:, :BR_W] + fre[:, BR_W:]) * scale
    him_ref[...] = jnp.where(is0, fim[:, :BR_W] + fim[:, BR_W:], fim[:, :BR_W] - fim[:, BR_W:]) * scale


def _filter_dft(L, tk, tabs, hfb):
    ch, sh, _ = tabs
    tab = pl.BlockSpec((tk, L), lambda j: (j, 0))
    return pl.pallas_call(
        functools.partial(_filter_dft_kernel, tk, L),
        out_shape=(jax.ShapeDtypeStruct((L, BR_W), F32), jax.ShapeDtypeStruct((L, BR_W), F32)),
        grid=(L // tk,),
        in_specs=[tab, tab, pl.BlockSpec((L, 2 * BR_W), lambda j: (0, 0))],
        out_specs=(pl.BlockSpec((tk, BR_W), lambda j: (j, 0)), pl.BlockSpec((tk, BR_W), lambda j: (j, 0))),
        compiler_params=_params(("parallel",)),
        name="hy_filter_dft",
    )(ch, sh, hfb)


def _dft_fwd_kernel(tk, ch, sh, u_ref, hre_ref, him_ref, y_ref):
    j = pl.program_id(1)
    uh = u_ref[:, :BR_W]
    ure = _dot(ch[...], uh)
    uim = _dot(sh[...], uh)
    hre = hre_ref[...]
    him = him_ref[...]
    row = lax.broadcasted_iota(jnp.int32, (tk, BR_W), 0) + j * tk
    is0 = row == 0
    y_ref[:, :BR_W] = jnp.where(is0, ure * hre, ure * hre - uim * him).astype(BF16)
    y_ref[:, BR_W:] = jnp.where(is0, uim * him, ure * him + uim * hre).astype(BF16)


def _dft_fwd(B, L, tk, tabs, u2, hre, him):
    ch, sh, _ = tabs
    nk = L // tk
    tab = pl.BlockSpec((tk, L), lambda b, j: (j, 0))
    hsp = pl.BlockSpec((tk, BR_W), lambda b, j: (j, 0))
    return pl.pallas_call(
        functools.partial(_dft_fwd_kernel, tk),
        out_shape=jax.ShapeDtypeStruct((B * L, 2 * BR_W), BF16),
        grid=(B, nk),
        in_specs=[tab, tab, pl.BlockSpec((L, 2 * BR_W), lambda b, j: (b, 0)), hsp, hsp],
        out_specs=pl.BlockSpec((tk, 2 * BR_W), lambda b, j: (b * nk + j, 0)),
        compiler_params=_params(("parallel", "parallel")),
        name="hy_dft_fwd",
    )(ch, sh, u2, hre, him)


def _dft_inv_kernel(ch, sth, y_ref, u_ref, x1g_ref, bias_ref, o_ref):
    conv = _dot(ch[...], y_ref[:, :BR_W]) + _dot(sth[...], y_ref[:, BR_W:])
    u2 = u_ref[:, :BR_W].astype(F32) + u_ref[:, BR_W:].astype(F32)
    o_ref[...] = (conv + u2 * bias_ref[...]) * x1g_ref[...]


def _dft_inv(B, L, tm, tabs, y2, u2, x1g, bias):
    ch, _, sth = tabs
    nt = L // tm
    tab = pl.BlockSpec((tm, L), lambda b, i: (i, 0))
    return pl.pallas_call(
        _dft_inv_kernel,
        out_shape=jax.ShapeDtypeStruct((B * L, BR_W), F32),
        grid=(B, nt),
        in_specs=[tab, tab,
                  pl.BlockSpec((L, 2 * BR_W), lambda b, i: (b, 0)),
                  pl.BlockSpec((tm, 2 * BR_W), lambda b, i: (b * nt + i, 0)),
                  pl.BlockSpec((tm, BR_W), lambda b, i: (b * nt + i, 0)),
                  pl.BlockSpec((1, BR_W), lambda b, i: (0, 0))],
        out_specs=pl.BlockSpec((tm, BR_W), lambda b, i: (b * nt + i, 0)),
        compiler_params=_params(("parallel", "parallel")),
        name="hy_dft_inv",
    )(ch, sth, y2, u2, x1g, bias.reshape(1, BR_W))


def _dft_tables(L):
    n = 2 * L
    k = jnp.arange(L, dtype=jnp.int32)
    m = (k[:, None] * k[None, :]) % n
    ang = m.astype(F32) * (2.0 * math.pi / n)
    c = jnp.cos(ang)
    s = -jnp.sin(ang)
    alt = jnp.where(k % 2 == 0, 1.0, -1.0).astype(F32)
    s = jnp.where((k == 0)[:, None], alt[None, :], s)
    s = s.astype(BF16)
    return c.astype(BF16), s, s.T


def _hy_positions(L):
    t = jnp.linspace(0.0, 1.0, L, dtype=F32)[:, None]
    w = 2.0 * math.pi * jnp.arange(L, dtype=F32)[:, None] / L
    bands = jnp.linspace(1e-4, HY_BANDS - 1, HY_BANDS, dtype=F32)[None, :]
    z = jnp.concatenate([t, jnp.cos(bands * w), -jnp.sin(bands * w)], axis=-1)
    return jnp.pad(z, ((0, 0), (0, LANES - HY_EMB)))


def _rw_prep_kernel(tps, rc, rp, rn, kc, kp, kn, vc, vp, vn, lc, lp, ln,
                    mu_r, mu_k, mu_v, mu_l, kkw, ka, rk, w0, a0, w2, a2, ones_bd,
                    r_out, nkk_out, bonus_out, wf, bf, kdf, wb, bb, kdb, vt_out):
    i = pl.program_id(0)
    first = (i % tps) == 0
    last = (i % tps) == tps - 1

    def shifted(cur_ref, prev_ref, next_ref, mu_ref):
        cur = cur_ref[...]
        up, dn = _neighbours(cur, prev_ref[...], next_ref[...], first, last)
        return cur + mu_ref[...] * (0.5 * (up + dn) - cur)

    r = shifted(rc, rp, rn, mu_r)
    k = shifted(kc, kp, kn, mu_k)
    v = shifted(vc, vp, vn, mu_v)
    lora = shifted(lc, lp, ln, mu_l)
    ones = ones_bd[...]

    kk = k * kkw[...]
    kk = kk * lax.rsqrt(_segsum(kk * kk, ones) + 1e-12)
    r_out[...] = r
    nkk_out[...] = -kk
    for q in range(ROW_TILE // LANES):
        for p in range(RW_H // 2):
            vt = v[q * LANES:(q + 1) * LANES, p * LANES:(p + 1) * LANES].T
            for s in range(LANES // SUB):
                vt_out[q * (LANES // SUB) + s, p] = jnp.concatenate(
                    [vt[:RW_N, s * SUB:(s + 1) * SUB], vt[RW_N:, s * SUB:(s + 1) * SUB]], axis=1)

    kd_sum = None
    outs = ((wf, bf, kdf), (wb, bb, kdb))
    for d in range(2):
        wl = lora[:, d * W_LORA:(d + 1) * W_LORA]
        al = lora[:, 2 * W_LORA + d * A_LORA:2 * W_LORA + (d + 1) * A_LORA]
        w = jnp.exp(-RW_DECAY_SCALE * _sigmoid(w0[d:d + 1, :] + _dot3(jnp.tanh(wl), w2[d])))
        a = _sigmoid(a0[d:d + 1, :] + _dot3(al, a2[d]))
        kd = k * (1.0 + (a - 1.0) * ka[...])
        outs[d][0][...] = w
        outs[d][1][...] = kk * a
        outs[d][2][...] = kd
        kd_sum = kd if kd_sum is None else kd_sum + kd
    bonus_out[...] = _segsum(r * (0.5 * kd_sum) * rk[...], ones) * v


def _rw_prep(z, p, ones_bd, tps):
    rows = z.shape[0]
    mu = p["rw_mu"]
    vec = lambda a: a.reshape(1, -1)
    full = lambda a: pl.BlockSpec(a.shape, lambda i: (0,) * a.ndim)
    smalls = [vec(mu[0:512]), vec(mu[512:1024]), vec(mu[1024:1536]), vec(mu[1536:1792]),
              vec(p["rw_kk"]), vec(p["rw_ka"]), vec(p["rw_rk"]), p["rw_w0"], p["rw_a0"],
              p["rw_w2"], p["rw_a2"], ones_bd]
    tile = pl.BlockSpec((ROW_TILE, BR_W), lambda i: (i, 0))
    npair = RW_H // 2
    vt_shape = jax.ShapeDtypeStruct((rows // SUB, npair, RW_N, LANES), F32)
    vt_spec = pl.BlockSpec((ROW_TILE // SUB, npair, RW_N, LANES), lambda i: (i, 0, 0, 0))
    return pl.pallas_call(
        functools.partial(_rw_prep_kernel, tps),
        out_shape=tuple(jax.ShapeDtypeStruct((rows, BR_W), F32) for _ in range(9)) + (vt_shape,),
        grid=(rows // ROW_TILE,),
        in_specs=(_halo_specs(BR_W, COL_RW_R, rows) + _halo_specs(BR_W, COL_RW_K, rows)
                  + _halo_specs(BR_W, COL_RW_V, rows) + _halo_specs(2 * LANES, COL_LORA, rows)
                  + [full(a) for a in smalls]),
        out_specs=tuple(tile for _ in range(9)) + (vt_spec,),
        compiler_params=_params(("parallel",)),
        name="rw_prep",
    )(*([z] * 12), *smalls)


def _scan_kernel(nc, bb, r_f, nk_f, w_f, b_f, kd_f, vt_f, r_b, nk_b, w_b, b_b, kd_b, vt_b,
                 s0, ones2, yt_f, yt_b, sfin, state):
    c = pl.program_id(1)
    npair = RW_H // 2
    nsub = SCAN_CHUNK // SUB
    chains = [(bi, d, p) for bi in range(bb) for d in range(2) for p in range(npair)]

    @pl.when(c == 0)
    def _():
        for ci, (bi, d, p) in enumerate(chains):
            state[ci] = jnp.concatenate([s0[bi, d, 2 * p], s0[bi, d, 2 * p + 1]], axis=1)

    g4 = ones2[...]
    lane_full = lax.broadcasted_iota(jnp.int32, (RW_N, LANES), 1)
    lane = lane_full & (SUB - 1)
    lane_head = lane_full & SUB
    dirs = ((r_f, nk_f, w_f, b_f, kd_f, vt_f, yt_f), (r_b, nk_b, w_b, b_b, kd_b, vt_b, yt_b))

    for sub in range(nsub):
        subs = (sub, nsub - 1 - sub)
        for bi, d, p in chains:
            dirs[d][6][bi, subs[d], p] = jnp.zeros((RW_N, LANES), F32)

        def body(g, carry):
            bases = (pl.multiple_of(g * HALO, HALO), pl.multiple_of((SUB // HALO - 1 - g) * HALO, HALO))
            for jj in range(HALO):
                js = (jj, HALO - 1 - jj)

                def row(ref, bi, d, p):
                    blk = ref[bi, pl.ds(subs[d] * SUB + bases[d], HALO), pl.ds(p * LANES, LANES)]
                    return blk[js[d]:js[d] + 1, :]

                def reduce_keys(tiles):
                    lhs = jnp.concatenate([jnp.concatenate(tiles[i:i + 2], axis=1) for i in range(0, len(tiles), 2)],
                                          axis=0)
                    res = _dot(lhs, g4)
                    return [res[(i // 2) * RW_N:(i // 2 + 1) * RW_N, (i % 2) * LANES:(i % 2 + 1) * LANES]
                            for i in range(len(tiles))]

                for g0 in range(0, len(chains), SCAN_GROUP):
                    group = list(enumerate(chains))[g0:g0 + SCAN_GROUP]
                    sas = reduce_keys([(state[ci] * row(dirs[d][1], bi, d, p)).astype(BF16)
                                       for ci, (bi, d, p) in group])
                    zr = []
                    for (ci, (bi, d, p)), sa in zip(group, sas):
                        tt = bases[d] + js[d]
                        vb = jnp.take_along_axis(dirs[d][5][bi, subs[d], p], lane_head + tt, axis=1)
                        st = (state[ci] * row(dirs[d][2], bi, d, p) + sa * row(dirs[d][3], bi, d, p)
                              + vb * row(dirs[d][4], bi, d, p))
                        state[ci] = st
                        zr.append((st * row(dirs[d][0], bi, d, p)).astype(BF16))
                    for (ci, (bi, d, p)), yb in zip(group, reduce_keys(zr)):
                        y_ref = dirs[d][6]
                        sel = lane == bases[d] + js[d]
                        y_ref[bi, subs[d], p] = jnp.where(sel, yb, y_ref[bi, subs[d], p])
            return carry

        lax.fori_loop(0, SUB // HALO, body, 0)

    @pl.when(c == nc - 1)
    def _():
        for ci, (bi, d, p) in enumerate(chains):
            sfin[bi, d, 2 * p] = state[ci][:, :RW_N]
            sfin[bi, d, 2 * p + 1] = state[ci][:, RW_N:]


def _rw_scan(B, L, seqs, vt, s0, ones2):
    r, nkk, wf, bf, kdf, wb, bwb, kdb = [a.reshape(B, L, BR_W) for a in seqs]
    bb = SCAN_BATCH
    nc = L // SCAN_CHUNK
    nsub = SCAN_CHUNK // SUB
    npair = RW_H // 2
    nch = bb * 2 * npair
    fwd = pl.BlockSpec((bb, SCAN_CHUNK, BR_W), lambda b, c: (b, c, 0))
    bwd = pl.BlockSpec((bb, SCAN_CHUNK, BR_W), lambda b, c: (b, nc - 1 - c, 0))
    tile = (bb, nsub, npair, RW_N, LANES)
    vfwd = pl.BlockSpec(tile, lambda b, c: (b, c, 0, 0, 0))
    vbwd = pl.BlockSpec(tile, lambda b, c: (b, nc - 1 - c, 0, 0, 0))
    sspec = pl.BlockSpec((bb, 2, RW_H, RW_N, RW_N), lambda b, c: (b, 0, 0, 0, 0))
    yshape = jax.ShapeDtypeStruct((B, L // SUB, npair, RW_N, LANES), F32)
    return pl.pallas_call(
        functools.partial(_scan_kernel, nc, bb),
        out_shape=(yshape, yshape, jax.ShapeDtypeStruct((B, 2, RW_H, RW_N, RW_N), F32)),
        grid=(B // bb, nc),
        in_specs=[fwd, fwd, fwd, fwd, fwd, vfwd, bwd, bwd, bwd, bwd, bwd, vbwd, sspec,
                  pl.BlockSpec((2 * LANES, 2 * LANES), lambda b, c: (0, 0))],
        out_specs=(vfwd, vbwd, sspec),
        scratch_shapes=[pltpu.VMEM((nch, RW_N, LANES), F32)],
        compiler_params=_params(("parallel", "arbitrary")),
        name="rw_scan",
    )(r, nkk, wf, bf, kdf, vt, r, nkk, wb, bwb, kdb, vt, s0, ones2)


def _rw_post_kernel(yf, yb, bonus, g_ref, lnw, lnb, ones_bd, o_ref, y_sc):
    ones = ones_bd[...]
    nsub = LANES // SUB
    for q in range(ROW_TILE // LANES):
        for p in range(RW_H // 2):
            tiles = [yf[q * nsub + s, p] + yb[q * nsub + s, p] for s in range(nsub)]
            heads = [jnp.concatenate([t[:, h * SUB:(h + 1) * SUB] for t in tiles], axis=1) for h in range(2)]
            y_sc[q * LANES:(q + 1) * LANES, p * LANES:(p + 1) * LANES] = jnp.concatenate(heads, axis=0).T
    y = y_sc[...]
    mu = _segsum(y, ones) * (1.0 / RW_N)
    yc = y - mu
    var = _segsum(yc * yc, ones) * (1.0 / RW_N)
    y = yc * lax.rsqrt(var + RW_GN_EPS) * lnw[...] + lnb[...]
    o_ref[...] = (y + bonus[...]) * _silu(g_ref[...])


def _rw_post(ytf, ytb, bonus, z, ln_w, ln_b, ones_bd):
    rows = bonus.shape[0]
    npair = RW_H // 2
    tile = pl.BlockSpec((ROW_TILE, BR_W), lambda i: (i, 0))
    ytile = pl.BlockSpec((ROW_TILE // SUB, npair, RW_N, LANES), lambda i: (i, 0, 0, 0))
    vec = pl.BlockSpec((1, BR_W), lambda i: (0, 0))
    return pl.pallas_call(
        _rw_post_kernel,
        out_shape=jax.ShapeDtypeStruct((rows, BR_W), F32),
        grid=(rows // ROW_TILE,),
        in_specs=[ytile, ytile, tile, pl.BlockSpec((ROW_TILE, BR_W), lambda i: (i, COL_RW_G // BR_W)),
                  vec, vec, pl.BlockSpec((BR_W, BR_W), lambda i: (0, 0))],
        out_specs=tile,
        scratch_shapes=[pltpu.VMEM((ROW_TILE, BR_W), F32)],
        compiler_params=_params(("parallel",)),
        name="rw_post",
    )(ytf, ytb, bonus, z, ln_w.reshape(1, BR_W), ln_b.reshape(1, BR_W), ones_bd)


def _rope(x, cos, sin_signed):
    lane = lax.broadcasted_iota(jnp.int32, x.shape, 1)
    low = (lane & (ROPE_AXIS - 1)) < (ROPE_AXIS // 2)
    width = x.shape[1]
    partner = jnp.where(low, pltpu.roll(x, width - ROPE_AXIS // 2, 1), pltpu.roll(x, ROPE_AXIS // 2, 1))
    return x * cos + partner * sin_signed


def _da_prep_kernel(rope, q_ref, k_ref, v_ref, gq, gk, ones_bd, *rest):
    if rope:
        cos_ref, sin_ref, q_out, k_out, v_out = rest
    else:
        q_out, k_out, v_out, kn_out = rest
    ones = ones_bd[...]
    v_out[...] = v_ref[...].astype(BF16)

    def norm(x, g):
        ms = _segsum(x * x, ones) * (1.0 / DA_DH)
        return x * lax.rsqrt(ms + NORM_EPS) * g

    qn = norm(q_ref[...], gq[...])
    kn = norm(k_ref[...], gk[...])
    if rope:
        qn = _rope(qn, cos_ref[...], sin_ref[...])
        kn = _rope(kn, cos_ref[...], sin_ref[...])
    else:
        kn_out[...] = kn
    k_out[...] = kn.astype(BF16)
    qn = qn * (DA_DH ** -0.5)
    lane = lax.broadcasted_iota(jnp.int32, (qn.shape[0], LANES), 1)
    for h in range(DA_H):
        qh = qn[:, h * LANES:(h + 1) * LANES]
        q_out[:, (2 * h) * LANES:(2 * h + 1) * LANES] = jnp.where(lane < DA_DH, qh, 0.0).astype(BF16)
        q_out[:, (2 * h + 1) * LANES:(2 * h + 2) * LANES] = jnp.where(lane >= DA_DH, qh, 0.0).astype(BF16)


def _da_prep(z, gq, gk, ones_bd, tps, rope_tabs):
    rows = z.shape[0]
    rope = rope_tabs is not None
    tile = lambda col: pl.BlockSpec((ROW_TILE, BR_W), lambda i: (i, col // BR_W))
    vec = pl.BlockSpec((1, BR_W), lambda i: (0, 0))
    in_specs = [tile(COL_DA_Q), tile(COL_DA_K), tile(COL_DA_V), vec, vec,
                pl.BlockSpec((BR_W, BR_W), lambda i: (0, 0))]
    args = [z, z, z, jnp.tile(gq, 2 * DA_H).reshape(1, BR_W), jnp.tile(gk, 2 * DA_H).reshape(1, BR_W), ones_bd]
    out_tile = pl.BlockSpec((ROW_TILE, BR_W), lambda i: (i, 0))
    out_shape = [jax.ShapeDtypeStruct((rows, 2 * BR_W), BF16), jax.ShapeDtypeStruct((rows, BR_W), BF16),
                 jax.ShapeDtypeStruct((rows, BR_W), BF16)]
    out_specs = [pl.BlockSpec((ROW_TILE, 2 * BR_W), lambda i: (i, 0)), out_tile, out_tile]
    if rope:
        tab = pl.BlockSpec((ROW_TILE, BR_W), lambda i: (i % tps, 0))
        in_specs += [tab, tab]
        args += list(rope_tabs)
    else:
        out_shape.append(jax.ShapeDtypeStruct((rows, BR_W), F32))
        out_specs.append(out_tile)
    return pl.pallas_call(
        functools.partial(_da_prep_kernel, rope),
        out_shape=tuple(out_shape),
        grid=(rows // ROW_TILE,),
        in_specs=in_specs,
        out_specs=tuple(out_specs),
        compiler_params=_params(("parallel",)),
        name="da_prep",
    )(*args)


def _rope_tables(L):
    half = ROPE_AXIS // 2
    lane = jnp.arange(BR_W)
    j = lane % DA_DH
    use_col = (j // ROPE_AXIS) == 1
    idx = (j % ROPE_AXIS) % half
    freqs = ROPE_BASE ** (-idx.astype(F32) / half)
    t = jnp.arange(L)
    pos = jnp.where(use_col[None, :], (t % GRID_W)[:, None], (t // GRID_W)[:, None]).astype(F32)
    ang = pos * freqs[None, :]
    sign = jnp.where((j % ROPE_AXIS) < half, -1.0, 1.0).astype(F32)
    return jnp.cos(ang), jnp.sin(ang) * sign[None, :]


def _da_attn_kernel(lam_init, has_ctx, q_ref, k_ref, v_ref, *rest):
    if has_ctx:
        kc_ref, vc_ref, g_ref, gsub, lq1, lk1, lq2, lk2, o_ref = rest
    else:
        g_ref, gsub, lq1, lk1, lq2, lk2, o_ref = rest
    lam = (jnp.exp(jnp.sum(lq1[...] * lk1[...], axis=-1, keepdims=True))
           - jnp.exp(jnp.sum(lq2[...] * lk2[...], axis=-1, keepdims=True)) + lam_init)
    nt = (((1,), (1,)), ((), ()))
    for h in range(DA_H):
        cols = slice(h * LANES, (h + 1) * LANES)
        parts = [(k_ref[:, cols], v_ref[:, cols])]
        if has_ctx:
            parts.append((kc_ref[0, :, cols], vc_ref[0, :, cols]))
        outs = []
        for m in range(2):
            q = q_ref[:, (2 * h + m) * LANES:(2 * h + m + 1) * LANES]
            scores = [lax.dot_general(q, kh, nt, preferred_element_type=F32) for kh, _ in parts]
            top = functools.reduce(jnp.maximum, [jnp.max(sc, axis=-1, keepdims=True) for sc in scores])
            num, den = 0.0, 0.0
            for sc, (_, vh) in zip(scores, parts):
                e = jnp.exp(sc - top)
                den = den + jnp.sum(e, axis=-1, keepdims=True)
                num = num + _dot(e.astype(BF16), vh)
            outs.append(num / den)
        o = outs[0] - lam * outs[1]
        o = o * lax.rsqrt(jnp.mean(o * o, axis=-1, keepdims=True) + 1e-5) * gsub[...] * (1.0 - lam_init)
        o_ref[:, cols] = o * _silu(g_ref[:, cols])


def _da_attn(B, L, lam_init, q2, k_bf, v_bf, ctx_kv, z, p):
    nt = L // ROW_TILE
    vec = lambda a: a.reshape(1, -1)
    small = lambda n: pl.BlockSpec((1, n), lambda b, i: (0, 0))
    seq = pl.BlockSpec((L, BR_W), lambda b, i: (b, 0))
    in_specs = [pl.BlockSpec((ROW_TILE, 2 * BR_W), lambda b, i: (b * nt + i, 0)), seq, seq]
    args = [q2, k_bf, v_bf]
    if ctx_kv is not None:
        past = ctx_kv[0].shape[1]
        cspec = pl.BlockSpec((1, past, BR_W), lambda b, i: (b, 0, 0))
        in_specs += [cspec, cspec]
        args += list(ctx_kv)
    in_specs += [pl.BlockSpec((ROW_TILE, BR_W), lambda b, i: (b * nt + i, COL_DA_G // BR_W)),
                 small(2 * DA_DH), small(DA_DH), small(DA_DH), small(DA_DH), small(DA_DH)]
    args += [z, vec(p["da_gsub"]), vec(p["da_lq1"]), vec(p["da_lk1"]), vec(p["da_lq2"]), vec(p["da_lk2"])]
    return pl.pallas_call(
        functools.partial(_da_attn_kernel, lam_init, ctx_kv is not None),
        out_shape=jax.ShapeDtypeStruct((B * L, BR_W), F32),
        grid=(B, nt),
        in_specs=in_specs,
        out_specs=pl.BlockSpec((ROW_TILE, BR_W), lambda b, i: (b * nt + i, 0)),
        compiler_params=_params(("parallel", "parallel")),
        name="da_attn",
    )(*args)


def _merge_kernel(ya, yb, yc, mga, mgb, mgc, x_ref, mod_ref, wbr, wout, o_ref):
    merged = (_sigmoid(mga[...]) * _dot(ya[...].astype(BF16), wbr[0])
              + _sigmoid(mgb[...]) * _dot(yb[...].astype(BF16), wbr[1])
              + _sigmoid(mgc[...]) * _dot(yc[...].astype(BF16), wbr[2]))
    out = _dot(merged.astype(BF16), wout[...])
    o_ref[...] = x_ref[...] + mod_ref[0][2:3] * out


def _merge(ya, yb, yc, z, x, mod, wbr_bf, wout_bf, tps):
    rows = x.shape[0]
    tile = pl.BlockSpec((ROW_TILE, BR_W), lambda i: (i, 0))
    mg = lambda n: pl.BlockSpec((ROW_TILE, D_MODEL), lambda i: (i, COL_MG // D_MODEL + n))
    return pl.pallas_call(
        _merge_kernel,
        out_shape=jax.ShapeDtypeStruct((rows, D_MODEL), F32),
        grid=(rows // ROW_TILE,),
        in_specs=[tile, tile, tile, mg(0), mg(1), mg(2),
                  pl.BlockSpec((ROW_TILE, D_MODEL), lambda i: (i, 0)),
                  pl.BlockSpec((1, 3, D_MODEL), lambda i: (i // tps, 0, 0)),
                  pl.BlockSpec((3, BR_W, D_MODEL), lambda i: (0, 0, 0)),
                  pl.BlockSpec((D_MODEL, D_MODEL), lambda i: (0, 0))],
        out_specs=pl.BlockSpec((ROW_TILE, D_MODEL), lambda i: (i, 0)),
        compiler_params=_params(("parallel",)),
        name="merge_out",
    )(ya, yb, yc, z, z, z, x, mod, wbr_bf, wout_bf)


def _permute_w_in(w):
    hy_z, hy_g = w[:, 0:1536], w[:, 1536:2048]
    rw_rkv, rw_lora, rw_g = w[:, 2048:3584], w[:, 3584:3840], w[:, 3840:4352]
    da = w[:, 4352:6400]
    mg = w[:, 6400:9472]
    return jnp.concatenate([hy_z, hy_g, rw_rkv, rw_g, da, mg, rw_lora], axis=1)


def _layer(x, mod, p, consts, B, L, lam_init, ctx):
    tps = L // ROW_TILE
    tabs, zpos, rope_tabs, ones_bd, ones2 = consts
    z = _inproj(x, mod, p["norm_g"], p["w_in_bf"], tps)

    u2, x1g = _hy_pre(z, p["hy_conv_w"], p["hy_conv_b"], tps)
    tk = min(L, 256)
    hfb = _hy_filters(L, zpos, p["hy_f1p"], p["hy_fb1"], p["hy_freq"], p["hy_f2"], p["hy_fb2"], p["hy_f3"])
    hre, him = _filter_dft(L, tk, tabs, hfb)
    spec = _dft_fwd(B, L, tk, tabs, u2, hre, him)
    y_a = _dft_inv(B, L, min(L, 512), tabs, spec, u2, x1g, p["hy_bias"])

    r, nkk, bonus, wf, bf, kdf, wb, bb, kdb, vt = _rw_prep(z, p, ones_bd, tps)
    npair = RW_H // 2
    s0 = jnp.zeros((B, 2, RW_H, RW_N, RW_N), F32) if ctx is None else ctx[2]
    tiles = (B, L // SUB, npair, RW_N, LANES)
    ytf, ytb, sfin = _rw_scan(B, L, (r, nkk, wf, bf, kdf, wb, bb, kdb), vt.reshape(tiles), s0, ones2)
    flat = (B * L // SUB, npair, RW_N, LANES)
    y_b = _rw_post(ytf.reshape(flat), ytb.reshape(flat), bonus, z, p["rw_ln_w"], p["rw_ln_b"], ones_bd)

    if ctx is None:
        q2, k_bf, v_bf, kn = _da_prep(z, p["da_gq"], p["da_gk"], ones_bd, tps, None)
        v_da = z[:, COL_DA_V:COL_DA_V + BR_W]
        ctx_kv = None
    else:
        q2, k_bf, v_bf = _da_prep(z, p["da_gq"], p["da_gk"], ones_bd, tps, rope_tabs)
        kn = v_da = None
        ctx_kv = (ctx[0].reshape(B, -1, BR_W).astype(BF16), ctx[1].reshape(B, -1, BR_W).astype(BF16))
    y_c = _da_attn(B, L, lam_init, q2, k_bf, v_bf, ctx_kv, z, p)

    x_new = _merge(y_a, y_b, y_c, z, x, mod, p["w_br_bf"], p["w_out_bf"], tps)
    return x_new, (kn, v_da, sfin)


def kernel(x_prompt, x_sample, cache_k, cache_v, state_rwkv, c, c_ctx, norm_g, w_ada, b_ada, w_in, hy_conv_w, hy_conv_b, hy_f1, hy_fb1, hy_freq, hy_f2, hy_fb2, hy_f3, hy_bias, rw_mu, rw_w0, rw_w2, rw_a0, rw_a2, rw_kk, rw_ka, rw_rk, rw_ln_w, rw_ln_b, da_gq, da_gk, da_lq1, da_lk1, da_lq2, da_lk2, da_gsub, w_br, w_out):
    Bp, Lp, _ = x_prompt.shape
    Bs, Ls, _ = x_sample.shape

    pad = (-(Bp + Bs)) % 8
    c_all = jnp.concatenate([jnp.broadcast_to(c_ctx, (Bp, D_MODEL)), c, jnp.zeros((pad, D_MODEL), F32)], axis=0)
    mod_all = _modulation(c_all, w_ada, b_ada)
    mod_p = mod_all[:, :Bp].reshape(DEPTH, Bp, 3, D_MODEL)
    mod_s = mod_all[:, Bp:Bp + Bs].reshape(DEPTH, Bs, 3, D_MODEL)

    seg = jnp.arange(BR_W) // RW_N
    ones_bd = (seg[:, None] == seg[None, :]).astype(BF16)
    ones2 = ones_bd[:2 * LANES, :2 * LANES]

    consts = {}
    for L in sorted({Lp, Ls}):
        consts[L] = (_dft_tables(L), _hy_positions(L), _rope_tables(L), ones_bd, ones2)

    def layer_params(l):
        return dict(norm_g=norm_g[l], w_in_bf=_permute_w_in(w_in[l]).astype(BF16),
                    hy_conv_w=hy_conv_w[l], hy_conv_b=hy_conv_b[l],
                    hy_f1p=jnp.pad(hy_f1[l], ((0, LANES - HY_EMB), (0, 0))), hy_fb1=hy_fb1[l],
                    hy_freq=hy_freq[l], hy_f2=hy_f2[l], hy_fb2=hy_fb2[l], hy_f3=hy_f3[l],
                    hy_bias=hy_bias[l], rw_mu=rw_mu[l], rw_w0=rw_w0[l], rw_w2=rw_w2[l],
                    rw_a0=rw_a0[l], rw_a2=rw_a2[l], rw_kk=rw_kk[l], rw_ka=rw_ka[l], rw_rk=rw_rk[l],
                    rw_ln_w=rw_ln_w[l], rw_ln_b=rw_ln_b[l], da_gq=da_gq[l], da_gk=da_gk[l],
                    da_lq1=da_lq1[l], da_lk1=da_lk1[l], da_lq2=da_lq2[l], da_lk2=da_lk2[l],
                    da_gsub=da_gsub[l], w_br_bf=w_br[l].astype(BF16), w_out_bf=w_out[l].astype(BF16))

    params = [layer_params(l) for l in range(DEPTH)]

    xp = x_prompt.reshape(Bp * Lp, D_MODEL)
    new_k, new_v, new_s = [], [], []
    for l in range(DEPTH):
        lam_init = 0.8 - 0.6 * math.exp(-0.3 * l)
        xp, (k_l, v_l, s_l) = _layer(xp, mod_p[l], params[l], consts[Lp], Bp, Lp, lam_init, None)
        new_k.append(k_l.reshape(Bp, Lp, DA_H, 2, DA_DH))
        new_v.append(v_l.reshape(Bp, Lp, DA_H, 2 * DA_DH))
        new_s.append(s_l)
    new_cache_k = jnp.stack(new_k, axis=1)
    new_cache_v = jnp.stack(new_v, axis=1)
    new_state = jnp.stack(new_s, axis=1)

    xs = x_sample.reshape(Bs * Ls, D_MODEL)
    for l in range(DEPTH):
        lam_init = 0.8 - 0.6 * math.exp(-0.3 * l)
        ctx = (cache_k[:, l], cache_v[:, l], state_rwkv[:, l])
        xs, _ = _layer(xs, mod_s[l], params[l], consts[Ls], Bs, Ls, lam_init, ctx)

    return (xp.reshape(Bp, Lp, D_MODEL), xs.reshape(Bs, Ls, D_MODEL), new_cache_k, new_cache_v, new_state)
```

```python
import functools
import math

import jax
import jax.numpy as jnp
from jax import lax
from jax.experimental import pallas as pl
from jax.experimental.pallas import tpu as pltpu

F32 = jnp.float32
BF16 = jnp.bfloat16

D_MODEL = 1024
DEPTH = 4
GRID_W = 64
BR_W = 512
HY_EMB = 33
HY_BANDS = (HY_EMB - 1) // 2
HY_FFN = 64
HY_MAX_DECAY = math.log(1e-2) / 0.3
HY_MIN_DECAY = math.log(1e-2) / 1.5
RW_N = 64
RW_H = BR_W // RW_N
W_LORA = 64
A_LORA = 64
RW_DECAY_SCALE = math.exp(-0.5)
RW_GN_EPS = 64e-5
DA_DH = 64
DA_H = BR_W // (2 * DA_DH)
ROPE_AXIS = DA_DH // 2
ROPE_BASE = 10000.0
NORM_EPS = 1e-6
N_IN = 9472

LANES = 128
ROW_TILE = 256
HALO = 8
Z_HALO = 16
SCAN_CHUNK = 128
SCAN_BATCH = 2
SCAN_GROUP = 16
SUB = 64
VMEM_LIMIT = 52 * 1024 * 1024

COL_HY_Z = 0
COL_HY_G = 1536
COL_RW_R = 2048
COL_RW_K = 2560
COL_RW_V = 3072
COL_RW_G = 3584
COL_DA_Q = 4096
COL_DA_K = 4608
COL_DA_V = 5120
COL_DA_G = 5632
COL_MG = 6144
COL_LORA = 9216


def _sigmoid(x):
    return 1.0 / (1.0 + jnp.exp(-x))


def _silu(x):
    return x * _sigmoid(x)


def _dot(a, b):
    return jnp.dot(a, b, preferred_element_type=F32)


def _split2(x):
    hi = x.astype(BF16)
    lo = (x - hi.astype(F32)).astype(BF16)
    return hi, lo


def _split3(x):
    hi = x.astype(BF16)
    r1 = x - hi.astype(F32)
    mid = r1.astype(BF16)
    lo = (r1 - mid.astype(F32)).astype(BF16)
    return hi, mid, lo


def _dot3(a, b):
    ah, al = _split2(a)
    bh, bl = _split2(b)
    return _dot(ah, bh) + _dot(ah, bl) + _dot(al, bh)


def _segsum(x, ones_bd):
    hi, mid, lo = _split3(x)
    return _dot(hi, ones_bd) + _dot(mid, ones_bd) + _dot(lo, ones_bd)


def _params(sem, vmem=VMEM_LIMIT):
    return pltpu.CompilerParams(dimension_semantics=sem, vmem_limit_bytes=vmem)


def _mod_kernel(c_ref, w_ref, b_ref, o_ref):
    c = c_ref[...]
    o_ref[0] = _dot3(_silu(c), w_ref[0]) + b_ref[0]


def _modulation(c_all, w_ada, b_ada):
    rows = c_all.shape[0]
    nt = 3 * D_MODEL // 1024
    return pl.pallas_call(
        _mod_kernel,
        out_shape=jax.ShapeDtypeStruct((DEPTH, rows, 3 * D_MODEL), F32),
        grid=(DEPTH, nt),
        in_specs=[pl.BlockSpec((rows, D_MODEL), lambda l, j: (0, 0)),
                  pl.BlockSpec((1, D_MODEL, 1024), lambda l, j: (l, 0, j)),
                  pl.BlockSpec((1, 1, 1024), lambda l, j: (l, 0, j))],
        out_specs=pl.BlockSpec((1, rows, 1024), lambda l, j: (l, 0, j)),
        compiler_params=_params(("parallel", "parallel")),
        name="adaln_mod",
    )(c_all, w_ada, b_ada.reshape(DEPTH, 1, 3 * D_MODEL))


def _inproj_kernel(x_ref, mod_ref, g_ref, w_ref, o_ref):
    x = x_ref[...]
    ms = jnp.mean(x * x, axis=-1, keepdims=True)
    y = x * lax.rsqrt(ms + NORM_EPS) * g_ref[...]
    mod = mod_ref[0]
    h = y * (1.0 + mod[1:2]) + mod[0:1]
    o_ref[...] = _dot(h.astype(BF16), w_ref[...]).astype(BF16)


def _inproj(x, mod, norm_g, w_in_bf, tps):
    rows = x.shape[0]
    half = N_IN // 2
    return pl.pallas_call(
        _inproj_kernel,
        out_shape=jax.ShapeDtypeStruct((rows, N_IN), BF16),
        grid=(2, rows // ROW_TILE),
        in_specs=[pl.BlockSpec((ROW_TILE, D_MODEL), lambda j, i: (i, 0)),
                  pl.BlockSpec((1, 3, D_MODEL), lambda j, i: (i // tps, 0, 0)),
                  pl.BlockSpec((1, D_MODEL), lambda j, i: (0, 0)),
                  pl.BlockSpec((D_MODEL, half), lambda j, i: (0, j))],
        out_specs=pl.BlockSpec((ROW_TILE, half), lambda j, i: (i, j)),
        compiler_params=_params(("parallel", "parallel")),
        name="inproj",
    )(x, mod, norm_g.reshape(1, D_MODEL), w_in_bf)


def _neighbours(cur, prev_blk, next_blk, first, last):
    rows = cur.shape[0]
    rid = lax.broadcasted_iota(jnp.int32, cur.shape, 0)
    prow = jnp.where(first, 0.0, prev_blk[Z_HALO - 1:Z_HALO, :])
    nrow = jnp.where(last, 0.0, next_blk[0:1, :])
    up = jnp.where(rid == 0, prow, pltpu.roll(cur, 1, 0))
    dn = jnp.where(rid == rows - 1, nrow, pltpu.roll(cur, rows - 1, 0))
    return up, dn


def _halo_specs(width, col, rows):
    per = ROW_TILE // Z_HALO
    nblk = rows // Z_HALO
    cb = col // width
    return [pl.BlockSpec((ROW_TILE, width), lambda i: (i, cb)),
            pl.BlockSpec((Z_HALO, width), lambda i: (jnp.maximum(i * per - 1, 0), cb)),
            pl.BlockSpec((Z_HALO, width), lambda i: (jnp.minimum((i + 1) * per, nblk - 1), cb))]


def _hy_pre_kernel(tps, zc, zp, zn, g_ref, cw, cb, u_ref, x1g_ref):
    i = pl.program_id(0)
    first = (i % tps) == 0
    last = (i % tps) == tps - 1
    z = zc[...].astype(F32)
    up, dn = _neighbours(z, zp[...].astype(F32), zn[...].astype(F32), first, last)
    u = up * cw[0:1, :] + z * cw[1:2, :] + dn * cw[2:3, :] + cb[...]
    v = u[:, :BR_W]
    x1 = u[:, BR_W:2 * BR_W]
    x2 = u[:, 2 * BR_W:]
    u2 = x2 * v
    hi, lo = _split2(u2)
    u_ref[:, :BR_W] = hi
    u_ref[:, BR_W:] = lo
    x1g_ref[...] = x1 * _silu(g_ref[...].astype(F32))


def _hy_pre(z, conv_w, conv_b, tps):
    rows = z.shape[0]
    return pl.pallas_call(
        functools.partial(_hy_pre_kernel, tps),
        out_shape=(jax.ShapeDtypeStruct((rows, 2 * BR_W), BF16),
                   jax.ShapeDtypeStruct((rows, BR_W), F32)),
        grid=(rows // ROW_TILE,),
        in_specs=_halo_specs(3 * BR_W, COL_HY_Z, rows) + [
            pl.BlockSpec((ROW_TILE, BR_W), lambda i: (i, COL_HY_G // BR_W)),
            pl.BlockSpec((3, 3 * BR_W), lambda i: (0, 0)),
            pl.BlockSpec((1, 3 * BR_W), lambda i: (0, 0))],
        out_specs=(pl.BlockSpec((ROW_TILE, 2 * BR_W), lambda i: (i, 0)),
                   pl.BlockSpec((ROW_TILE, BR_W), lambda i: (i, 0))),
        compiler_params=_params(("parallel",)),
        name="hy_pre",
    )(z, z, z, z, conv_w, conv_b.reshape(1, 3 * BR_W))


def _filter_kernel(L, zp, f1, fb1, fr, f2, fb2, f3, o_ref):
    freq = fr[...]
    hid = jnp.sin(freq * (_dot3(zp[...], f1[...]) + fb1[...]))
    hid = jnp.sin(freq * (_dot3(hid, f2[...]) + fb2[...]))
    h = _dot3(hid, f3[...])
    row = lax.broadcasted_iota(jnp.int32, (L, BR_W), 0)
    lane = lax.broadcasted_iota(jnp.int32, (L, BR_W), 1)
    t = row.astype(F32) / float(L - 1)
    delta = HY_MIN_DECAY + lane.astype(F32) * ((HY_MAX_DECAY - HY_MIN_DECAY) / float(BR_W - 1))
    win = jnp.exp(-t * jnp.abs(delta))
    hf = h[:, :BR_W] * win
    hb = jnp.where(row == 0, 0.0, h[:, BR_W:] * win)
    o_ref[:, :BR_W] = hf.astype(BF16)
    o_ref[:, BR_W:] = hb.astype(BF16)


def _hy_filters(L, zpos, f1p, fb1, freq, f2, fb2, f3):
    return pl.pallas_call(
        functools.partial(_filter_kernel, L),
        out_shape=jax.ShapeDtypeStruct((L, 2 * BR_W), BF16),
        compiler_params=_params(None),
        name="hy_filter",
    )(zpos, f1p, fb1.reshape(1, HY_FFN), freq.reshape(1, HY_FFN), f2, fb2.reshape(1, HY_FFN), f3)


def _filter_dft_kernel(tk, L, ch, sh, hfb, hre_ref, him_ref):
    j = pl.program_id(0)
    fre = _dot(ch[...], hfb[...])
    fim = _dot(sh[...], hfb[...])
    row = lax.broadcasted_iota(jnp.int32, (tk, BR_W), 0) + j * tk
    is0 = row == 0
    scale = jnp.where(is0, 1.0 / (2 * L), 2.0 / (2 * L))
    hre_ref[...] = (fre[:, :BR_W] + fre[:, BR_W:]) * scale
    him_ref[...] = jnp.where(is0, fim[:, :BR_W] + fim[:, BR_W:], fim[:, :BR_W] - fim[:, BR_W:]) * scale


def _filter_dft(L, tk, tabs, hfb):
    ch, sh, _ = tabs
    tab = pl.BlockSpec((tk, L), lambda j: (j, 0))
    return pl.pallas_call(
        functools.partial(_filter_dft_kernel, tk, L),
        out_shape=(jax.ShapeDtypeStruct((L, BR_W), F32), jax.ShapeDtypeStruct((L, BR_W), F32)),
        grid=(L // tk,),
        in_specs=[tab, tab, pl.BlockSpec((L, 2 * BR_W), lambda j: (0, 0))],
        out_specs=(pl.BlockSpec((tk, BR_W), lambda j: (j, 0)), pl.BlockSpec((tk, BR_W), lambda j: (j, 0))),
        compiler_params=_params(("parallel",)),
        name="hy_filter_dft",
    )(ch, sh, hfb)


def _dft_fwd_kernel(tk, ch, sh, u_ref, hre_ref, him_ref, y_ref):
    j = pl.program_id(1)
    uh = u_ref[:, :BR_W]
    ure = _dot(ch[...], uh)
    uim = _dot(sh[...], uh)
    hre = hre_ref[...]
    him = him_ref[...]
    row = lax.broadcasted_iota(jnp.int32, (tk, BR_W), 0) + j * tk
    is0 = row == 0
    y_ref[:, :BR_W] = jnp.where(is0, ure * hre, ure * hre - uim * him).astype(BF16)
    y_ref[:, BR_W:] = jnp.where(is0, uim * him, ure * him + uim * hre).astype(BF16)


def _dft_fwd(B, L, tk, tabs, u2, hre, him):
    ch, sh, _ = tabs
    nk = L // tk
    tab = pl.BlockSpec((tk, L), lambda b, j: (j, 0))
    hsp = pl.BlockSpec((tk, BR_W), lambda b, j: (j, 0))
    return pl.pallas_call(
        functools.partial(_dft_fwd_kernel, tk),
        out_shape=jax.ShapeDtypeStruct((B * L, 2 * BR_W), BF16),
        grid=(B, nk),
        in_specs=[tab, tab, pl.BlockSpec((L, 2 * BR_W), lambda b, j: (b, 0)), hsp, hsp],
        out_specs=pl.BlockSpec((tk, 2 * BR_W), lambda b, j: (b * nk + j, 0)),
        compiler_params=_params(("parallel", "parallel")),
        name="hy_dft_fwd",
    )(ch, sh, u2, hre, him)


def _dft_inv_kernel(ch, sth, y_ref, u_ref, x1g_ref, bias_ref, o_ref):
    conv = _dot(ch[...], y_ref[:, :BR_W]) + _dot(sth[...], y_ref[:, BR_W:])
    u2 = u_ref[:, :BR_W].astype(F32) + u_ref[:, BR_W:].astype(F32)
    o_ref[...] = (conv + u2 * bias_ref[...]) * x1g_ref[...]


def _dft_inv(B, L, tm, tabs, y2, u2, x1g, bias):
    ch, _, sth = tabs
    nt = L // tm
    tab = pl.BlockSpec((tm, L), lambda b, i: (i, 0))
    return pl.pallas_call(
        _dft_inv_kernel,
        out_shape=jax.ShapeDtypeStruct((B * L, BR_W), F32),
        grid=(B, nt),
        in_specs=[tab, tab,
                  pl.BlockSpec((L, 2 * BR_W), lambda b, i: (b, 0)),
                  pl.BlockSpec((tm, 2 * BR_W), lambda b, i: (b * nt + i, 0)),
                  pl.BlockSpec((tm, BR_W), lambda b, i: (b * nt + i, 0)),
                  pl.BlockSpec((1, BR_W), lambda b, i: (0, 0))],
        out_specs=pl.BlockSpec((tm, BR_W), lambda b, i: (b * nt + i, 0)),
        compiler_params=_params(("parallel", "parallel")),
        name="hy_dft_inv",
    )(ch, sth, y2, u2, x1g, bias.reshape(1, BR_W))


def _dft_tables(L):
    n = 2 * L
    k = jnp.arange(L, dtype=jnp.int32)
    m = (k[:, None] * k[None, :]) % n
    ang = m.astype(F32) * (2.0 * math.pi / n)
    c = jnp.cos(ang)
    s = -jnp.sin(ang)
    alt = jnp.where(k % 2 == 0, 1.0, -1.0).astype(F32)
    s = jnp.where((k == 0)[:, None], alt[None, :], s)
    s = s.astype(BF16)
    return c.astype(BF16), s, s.T


def _hy_positions(L):
    t = jnp.linspace(0.0, 1.0, L, dtype=F32)[:, None]
    w = 2.0 * math.pi * jnp.arange(L, dtype=F32)[:, None] / L
    bands = jnp.linspace(1e-4, HY_BANDS - 1, HY_BANDS, dtype=F32)[None, :]
    z = jnp.concatenate([t, jnp.cos(bands * w), -jnp.sin(bands * w)], axis=-1)
    return jnp.pad(z, ((0, 0), (0, LANES - HY_EMB)))


def _rw_prep_kernel(tps, rc, rp, rn, kc, kp, kn, vc, vp, vn, lc, lp, ln,
                    mu_r, mu_k, mu_v, mu_l, kkw, ka, rk, w0, a0, w2, a2, ones_bd,
                    r_out, nkk_out, bonus_out, wf, bf, kdf, wb, bb, kdb, vt_out):
    i = pl.program_id(0)
    first = (i % tps) == 0
    last = (i % tps) == tps - 1

    def shifted(cur_ref, prev_ref, next_ref, mu_ref):
        cur = cur_ref[...].astype(F32)
        up, dn = _neighbours(cur, prev_ref[...].astype(F32), next_ref[...].astype(F32), first, last)
        return cur + mu_ref[...] * (0.5 * (up + dn) - cur)

    r = shifted(rc, rp, rn, mu_r)
    k = shifted(kc, kp, kn, mu_k)
    v = shifted(vc, vp, vn, mu_v)
    lora = shifted(lc, lp, ln, mu_l)
    ones = ones_bd[...]

    kk = k * kkw[...]
    kk = kk * lax.rsqrt(_segsum(kk * kk, ones) + 1e-12)
    r_out[...] = r
    nkk_out[...] = -kk
    for q in range(ROW_TILE // LANES):
        for p in range(RW_H // 2):
            vt = v[q * LANES:(q + 1) * LANES, p * LANES:(p + 1) * LANES].T
            for s in range(LANES // SUB):
                vt_out[q * (LANES // SUB) + s, p] = jnp.concatenate(
                    [vt[:RW_N, s * SUB:(s + 1) * SUB], vt[RW_N:, s * SUB:(s + 1) * SUB]], axis=1)

    kd_sum = None
    outs = ((wf, bf, kdf), (wb, bb, kdb))
    for d in range(2):
        wl = lora[:, d * W_LORA:(d + 1) * W_LORA]
        al = lora[:, 2 * W_LORA + d * A_LORA:2 * W_LORA + (d + 1) * A_LORA]
        w = jnp.exp(-RW_DECAY_SCALE * _sigmoid(w0[d:d + 1, :] + _dot3(jnp.tanh(wl), w2[d])))
        a = _sigmoid(a0[d:d + 1, :] + _dot3(al, a2[d]))
        kd = k * (1.0 + (a - 1.0) * ka[...])
        outs[d][0][...] = w
        outs[d][1][...] = kk * a
        outs[d][2][...] = kd
        kd_sum = kd if kd_sum is None else kd_sum + kd
    bonus_out[...] = _segsum(r * (0.5 * kd_sum) * rk[...], ones) * v


def _rw_prep(z, p, ones_bd, tps):
    rows = z.shape[0]
    mu = p["rw_mu"]
    vec = lambda a: a.reshape(1, -1)
    full = lambda a: pl.BlockSpec(a.shape, lambda i: (0,) * a.ndim)
    smalls = [vec(mu[0:512]), vec(mu[512:1024]), vec(mu[1024:1536]), vec(mu[1536:1792]),
              vec(p["rw_kk"]), vec(p["rw_ka"]), vec(p["rw_rk"]), p["rw_w0"], p["rw_a0"],
              p["rw_w2"], p["rw_a2"], ones_bd]
    tile = pl.BlockSpec((ROW_TILE, BR_W), lambda i: (i, 0))
    npair = RW_H // 2
    vt_shape = jax.ShapeDtypeStruct((rows // SUB, npair, RW_N, LANES), F32)
    vt_spec = pl.BlockSpec((ROW_TILE // SUB, npair, RW_N, LANES), lambda i: (i, 0, 0, 0))
    return pl.pallas_call(
        functools.partial(_rw_prep_kernel, tps),
        out_shape=tuple(jax.ShapeDtypeStruct((rows, BR_W), F32) for _ in range(9)) + (vt_shape,),
        grid=(rows // ROW_TILE,),
        in_specs=(_halo_specs(BR_W, COL_RW_R, rows) + _halo_specs(BR_W, COL_RW_K, rows)
                  + _halo_specs(BR_W, COL_RW_V, rows) + _halo_specs(2 * LANES, COL_LORA, rows)
                  + [full(a) for a in smalls]),
        out_specs=tuple(tile for _ in range(9)) + (vt_spec,),
        compiler_params=_params(("parallel",)),
        name="rw_prep",
    )(*([z] * 12), *smalls)


def _scan_kernel(nc, bb, r_f, nk_f, w_f, b_f, kd_f, vt_f, r_b, nk_b, w_b, b_b, kd_b, vt_b,
                 s0, ones2, _, yt_f, yt_b, sfin, state):
    c = pl.program_id(1)
    npair = RW_H // 2
    nsub = SCAN_CHUNK // SUB
    chains = [(bi, d, p) for bi in range(bb) for d in range(2) for p in range(npair)]

    @pl.when(c == 0)
    def _():
        for ci, (bi, d, p) in enumerate(chains):
            state[ci] = jnp.concatenate([s0[bi, d, 2 * p], s0[bi, d, 2 * p + 1]], axis=1)

    g4 = ones2[...]
    lane_full = lax.broadcasted_iota(jnp.int32, (RW_N, LANES), 1)
    lane = lane_full & (SUB - 1)
    lane_head = lane_full & SUB
    dirs = ((r_f, nk_f, w_f, b_f, kd_f, vt_f, yt_f), (r_b, nk_b, w_b, b_b, kd_b, vt_b, yt_b))

    for sub in range(nsub):
        subs = (sub, nsub - 1 - sub)
        for bi, d, p in chains:
            dirs[d][6][bi, subs[d], p] = jnp.zeros((RW_N, LANES), F32)

        def body(g, carry):
            bases = (pl.multiple_of(g * HALO, HALO), pl.multiple_of((SUB // HALO - 1 - g) * HALO, HALO))

            for jj in range(HALO):
                js = (jj, HALO - 1 - jj)

                def row(ref, bi, d, p):
                    blk = ref[bi, pl.ds(subs[d] * SUB + bases[d], HALO), pl.ds(p * LANES, LANES)]
                    return blk[js[d]:js[d] + 1, :]

                def reduce_keys(tiles):
                    lhs = jnp.concatenate([jnp.concatenate(tiles[i:i + 2], axis=1) for i in range(0, len(tiles), 2)],
                                          axis=0)
                    res = _dot(lhs, g4)
                    return [res[(i // 2) * RW_N:(i // 2 + 1) * RW_N, (i % 2) * LANES:(i % 2 + 1) * LANES]
                            for i in range(len(tiles))]

                for g0 in range(0, len(chains), SCAN_GROUP):
                    group = list(enumerate(chains))[g0:g0 + SCAN_GROUP]
                    sas = reduce_keys([state[ci].astype(BF16) * row(dirs[d][1], bi, d, p).astype(BF16)
                                       for ci, (bi, d, p) in group])
                    zr = []
                    for (ci, (bi, d, p)), sa in zip(group, sas):
                        tt = bases[d] + js[d]
                        vb = jnp.take_along_axis(dirs[d][5][bi, subs[d], p], lane_head + tt, axis=1)
                        st = (state[ci] * row(dirs[d][2], bi, d, p) + sa * row(dirs[d][3], bi, d, p)
                              + vb * row(dirs[d][4], bi, d, p))
                        state[ci] = st
                        zr.append(st.astype(BF16) * row(dirs[d][0], bi, d, p).astype(BF16))
                    for (ci, (bi, d, p)), yb in zip(group, reduce_keys(zr)):
                        y_ref = dirs[d][6]
                        sel = lane == bases[d] + js[d]
                        y_ref[bi, subs[d], p] = jnp.where(sel, yb, y_ref[bi, subs[d], p])
            return carry

        lax.fori_loop(0, SUB // HALO, body, 0)

    @pl.when(c == nc - 1)
    def _():
        for ci, (bi, d, p) in enumerate(chains):
            sfin[bi, 0, d, 2 * p] = state[ci][:, :RW_N]
            sfin[bi, 0, d, 2 * p + 1] = state[ci][:, RW_N:]


def _rw_scan(B, L, seqs, vt, s0, ones2, sbuf, layer):
    r, nkk, wf, bf, kdf, wb, bwb, kdb = [a.reshape(B, L, BR_W) for a in seqs]
    bb = SCAN_BATCH
    nc = L // SCAN_CHUNK
    nsub = SCAN_CHUNK // SUB
    npair = RW_H // 2
    nch = bb * 2 * npair
    fwd = pl.BlockSpec((bb, SCAN_CHUNK, BR_W), lambda b, c: (b, c, 0))
    bwd = pl.BlockSpec((bb, SCAN_CHUNK, BR_W), lambda b, c: (b, nc - 1 - c, 0))
    tile = (bb, nsub, npair, RW_N, LANES)
    vfwd = pl.BlockSpec(tile, lambda b, c: (b, c, 0, 0, 0))
    vbwd = pl.BlockSpec(tile, lambda b, c: (b, nc - 1 - c, 0, 0, 0))
    sspec = pl.BlockSpec((bb, 2, RW_H, RW_N, RW_N), lambda b, c: (b, 0, 0, 0, 0))
    yshape = jax.ShapeDtypeStruct((B, L // SUB, npair, RW_N, LANES), F32)
    fspec = pl.BlockSpec((bb, 1, 2, RW_H, RW_N, RW_N), lambda b, c: (b, layer, 0, 0, 0, 0))
    return pl.pallas_call(
        functools.partial(_scan_kernel, nc, bb),
        out_shape=(yshape, yshape, jax.ShapeDtypeStruct(sbuf.shape, F32)),
        grid=(B // bb, nc),
        in_specs=[fwd, fwd, fwd, fwd, fwd, vfwd, bwd, bwd, bwd, bwd, bwd, vbwd, sspec,
                  pl.BlockSpec((2 * LANES, 2 * LANES), lambda b, c: (0, 0)),
                  pl.BlockSpec(memory_space=pl.ANY)],
        out_specs=(vfwd, vbwd, fspec),
        scratch_shapes=[pltpu.VMEM((nch, RW_N, LANES), F32)],
        input_output_aliases={14: 2},
        compiler_params=_params(("parallel", "arbitrary")),
        name="rw_scan",
    )(r, nkk, wf, bf, kdf, vt, r, nkk, wb, bwb, kdb, vt, s0, ones2, sbuf)


def _rw_post_kernel(yf, yb, bonus, g_ref, lnw, lnb, ones_bd, o_ref, y_sc):
    ones = ones_bd[...]
    nsub = LANES // SUB
    for q in range(ROW_TILE // LANES):
        for p in range(RW_H // 2):
            tiles = [yf[q * nsub + s, p] + yb[q * nsub + s, p] for s in range(nsub)]
            heads = [jnp.concatenate([t[:, h * SUB:(h + 1) * SUB] for t in tiles], axis=1) for h in range(2)]
            y_sc[q * LANES:(q + 1) * LANES, p * LANES:(p + 1) * LANES] = jnp.concatenate(heads, axis=0).T
    y = y_sc[...]
    mu = _segsum(y, ones) * (1.0 / RW_N)
    yc = y - mu
    var = _segsum(yc * yc, ones) * (1.0 / RW_N)
    y = yc * lax.rsqrt(var + RW_GN_EPS) * lnw[...] + lnb[...]
    o_ref[...] = (y + bonus[...]) * _silu(g_ref[...].astype(F32))


def _rw_post(ytf, ytb, bonus, z, ln_w, ln_b, ones_bd):
    rows = bonus.shape[0]
    npair = RW_H // 2
    tile = pl.BlockSpec((ROW_TILE, BR_W), lambda i: (i, 0))
    ytile = pl.BlockSpec((ROW_TILE // SUB, npair, RW_N, LANES), lambda i: (i, 0, 0, 0))
    vec = pl.BlockSpec((1, BR_W), lambda i: (0, 0))
    return pl.pallas_call(
        _rw_post_kernel,
        out_shape=jax.ShapeDtypeStruct((rows, BR_W), F32),
        grid=(rows // ROW_TILE,),
        in_specs=[ytile, ytile, tile, pl.BlockSpec((ROW_TILE, BR_W), lambda i: (i, COL_RW_G // BR_W)),
                  vec, vec, pl.BlockSpec((BR_W, BR_W), lambda i: (0, 0))],
        out_specs=tile,
        scratch_shapes=[pltpu.VMEM((ROW_TILE, BR_W), F32)],
        compiler_params=_params(("parallel",)),
        name="rw_post",
    )(ytf, ytb, bonus, z, ln_w.reshape(1, BR_W), ln_b.reshape(1, BR_W), ones_bd)


def _rope(x, cos, sin_signed):
    lane = lax.broadcasted_iota(jnp.int32, x.shape, 1)
    low = (lane & (ROPE_AXIS - 1)) < (ROPE_AXIS // 2)
    width = x.shape[1]
    partner = jnp.where(low, pltpu.roll(x, width - ROPE_AXIS // 2, 1), pltpu.roll(x, ROPE_AXIS // 2, 1))
    return x * cos + partner * sin_signed


def _da_prep_kernel(rope, q_ref, k_ref, v_ref, gq, gk, ones_bd, *rest):
    if rope:
        cos_ref, sin_ref, q_out, k_out, v_out = rest
    else:
        _, _, q_out, k_out, v_out, kn_out, vc_out = rest
        vc_out[0, 0] = v_ref[...].astype(F32)
    ones = ones_bd[...]
    v_out[...] = v_ref[...]

    def norm(x, g):
        ms = _segsum(x * x, ones) * (1.0 / DA_DH)
        return x * lax.rsqrt(ms + NORM_EPS) * g

    qn = norm(q_ref[...].astype(F32), gq[...])
    kn = norm(k_ref[...].astype(F32), gk[...])
    if rope:
        qn = _rope(qn, cos_ref[...], sin_ref[...])
        kn = _rope(kn, cos_ref[...], sin_ref[...])
    else:
        kn_out[0, 0] = kn
    k_out[...] = kn.astype(BF16)
    qn = qn * (DA_DH ** -0.5)
    lane = lax.broadcasted_iota(jnp.int32, (qn.shape[0], LANES), 1)
    for h in range(DA_H):
        qh = qn[:, h * LANES:(h + 1) * LANES]
        q_out[:, (2 * h) * LANES:(2 * h + 1) * LANES] = jnp.where(lane < DA_DH, qh, 0.0).astype(BF16)
        q_out[:, (2 * h + 1) * LANES:(2 * h + 2) * LANES] = jnp.where(lane >= DA_DH, qh, 0.0).astype(BF16)


def _da_prep(z, gq, gk, ones_bd, tps, rope_tabs, caches=None, layer=0):
    rows = z.shape[0]
    rope = rope_tabs is not None
    aliases = {}
    tile = lambda col: pl.BlockSpec((ROW_TILE, BR_W), lambda i: (i, col // BR_W))
    vec = pl.BlockSpec((1, BR_W), lambda i: (0, 0))
    in_specs = [tile(COL_DA_Q), tile(COL_DA_K), tile(COL_DA_V), vec, vec,
                pl.BlockSpec((BR_W, BR_W), lambda i: (0, 0))]
    args = [z, z, z, jnp.tile(gq, 2 * DA_H).reshape(1, BR_W), jnp.tile(gk, 2 * DA_H).reshape(1, BR_W), ones_bd]
    out_tile = pl.BlockSpec((ROW_TILE, BR_W), lambda i: (i, 0))
    out_shape = [jax.ShapeDtypeStruct((rows, 2 * BR_W), BF16), jax.ShapeDtypeStruct((rows, BR_W), BF16),
                 jax.ShapeDtypeStruct((rows, BR_W), BF16)]
    out_specs = [pl.BlockSpec((ROW_TILE, 2 * BR_W), lambda i: (i, 0)), out_tile, out_tile]
    if rope:
        tab = pl.BlockSpec((ROW_TILE, BR_W), lambda i: (i % tps, 0))
        in_specs += [tab, tab]
        args += list(rope_tabs)
    else:
        cspec = pl.BlockSpec((1, 1, ROW_TILE, BR_W), lambda i: (i // tps, layer, i % tps, 0))
        for buf in caches:
            aliases[len(args)] = len(out_shape)
            in_specs.append(pl.BlockSpec(memory_space=pl.ANY))
            args.append(buf)
            out_shape.append(jax.ShapeDtypeStruct(buf.shape, buf.dtype))
            out_specs.append(cspec)
    return pl.pallas_call(
        functools.partial(_da_prep_kernel, rope),
        out_shape=tuple(out_shape),
        grid=(rows // ROW_TILE,),
        in_specs=in_specs,
        out_specs=tuple(out_specs),
        input_output_aliases=aliases,
        compiler_params=_params(("parallel",)),
        name="da_prep",
    )(*args)


def _rope_tables(L):
    half = ROPE_AXIS // 2
    lane = jnp.arange(BR_W)
    j = lane % DA_DH
    use_col = (j // ROPE_AXIS) == 1
    idx = (j % ROPE_AXIS) % half
    freqs = ROPE_BASE ** (-idx.astype(F32) / half)
    t = jnp.arange(L)
    pos = jnp.where(use_col[None, :], (t % GRID_W)[:, None], (t // GRID_W)[:, None]).astype(F32)
    ang = pos * freqs[None, :]
    sign = jnp.where((j % ROPE_AXIS) < half, -1.0, 1.0).astype(F32)
    return jnp.cos(ang), jnp.sin(ang) * sign[None, :]


def _da_attn_kernel(lam_init, has_ctx, q_ref, k_ref, v_ref, *rest):
    if has_ctx:
        kc_ref, vc_ref, g_ref, gsub, lq1, lk1, lq2, lk2, o_ref = rest
    else:
        g_ref, gsub, lq1, lk1, lq2, lk2, o_ref = rest
    lam = (jnp.exp(jnp.sum(lq1[...] * lk1[...], axis=-1, keepdims=True))
           - jnp.exp(jnp.sum(lq2[...] * lk2[...], axis=-1, keepdims=True)) + lam_init)
    nt = (((1,), (1,)), ((), ()))
    for h in range(DA_H):
        cols = slice(h * LANES, (h + 1) * LANES)
        parts = [(k_ref[:, cols], v_ref[:, cols])]
        if has_ctx:
            parts.append((kc_ref[0, :, cols], vc_ref[0, :, cols]))
        outs = []
        for m in range(2):
            q = q_ref[:, (2 * h + m) * LANES:(2 * h + m + 1) * LANES]
            scores = [lax.dot_general(q, kh, nt, preferred_element_type=F32) for kh, _ in parts]
            top = functools.reduce(jnp.maximum, [jnp.max(sc, axis=-1, keepdims=True) for sc in scores])
            num, den = 0.0, 0.0
            for sc, (_, vh) in zip(scores, parts):
                e = jnp.exp(sc - top)
                den = den + jnp.sum(e, axis=-1, keepdims=True)
                num = num + _dot(e.astype(BF16), vh)
            outs.append(num / den)
        o = outs[0] - lam * outs[1]
        o = o * lax.rsqrt(jnp.mean(o * o, axis=-1, keepdims=True) + 1e-5) * gsub[...] * (1.0 - lam_init)
        o_ref[:, cols] = o * _silu(g_ref[:, cols].astype(F32))


def _da_attn(B, L, lam_init, q2, k_bf, v_bf, ctx_kv, z, p):
    nt = L // ROW_TILE
    vec = lambda a: a.reshape(1, -1)
    small = lambda n: pl.BlockSpec((1, n), lambda b, i: (0, 0))
    seq = pl.BlockSpec((L, BR_W), lambda b, i: (b, 0))
    in_specs = [pl.BlockSpec((ROW_TILE, 2 * BR_W), lambda b, i: (b * nt + i, 0)), seq, seq]
    args = [q2, k_bf, v_bf]
    if ctx_kv is not None:
        past = ctx_kv[0].shape[1]
        cspec = pl.BlockSpec((1, past, BR_W), lambda b, i: (b, 0, 0))
        in_specs += [cspec, cspec]
        args += list(ctx_kv)
    in_specs += [pl.BlockSpec((ROW_TILE, BR_W), lambda b, i: (b * nt + i, COL_DA_G // BR_W)),
                 small(2 * DA_DH), small(DA_DH), small(DA_DH), small(DA_DH), small(DA_DH)]
    args += [z, vec(p["da_gsub"]), vec(p["da_lq1"]), vec(p["da_lk1"]), vec(p["da_lq2"]), vec(p["da_lk2"])]
    return pl.pallas_call(
        functools.partial(_da_attn_kernel, lam_init, ctx_kv is not None),
        out_shape=jax.ShapeDtypeStruct((B * L, BR_W), F32),
        grid=(B, nt),
        in_specs=in_specs,
        out_specs=pl.BlockSpec((ROW_TILE, BR_W), lambda b, i: (b * nt + i, 0)),
        compiler_params=_params(("parallel", "parallel")),
        name="da_attn",
    )(*args)


def _merge_kernel(ya, yb, yc, mga, mgb, mgc, x_ref, mod_ref, wbr, wout, o_ref):
    merged = (_sigmoid(mga[...].astype(F32)) * _dot(ya[...].astype(BF16), wbr[0])
              + _sigmoid(mgb[...].astype(F32)) * _dot(yb[...].astype(BF16), wbr[1])
              + _sigmoid(mgc[...].astype(F32)) * _dot(yc[...].astype(BF16), wbr[2]))
    out = _dot(merged.astype(BF16), wout[...])
    o_ref[...] = x_ref[...] + mod_ref[0][2:3] * out


def _merge(ya, yb, yc, z, x, mod, wbr_bf, wout_bf, tps):
    rows = x.shape[0]
    tile = pl.BlockSpec((ROW_TILE, BR_W), lambda i: (i, 0))
    mg = lambda n: pl.BlockSpec((ROW_TILE, D_MODEL), lambda i: (i, COL_MG // D_MODEL + n))
    return pl.pallas_call(
        _merge_kernel,
        out_shape=jax.ShapeDtypeStruct((rows, D_MODEL), F32),
        grid=(rows // ROW_TILE,),
        in_specs=[tile, tile, tile, mg(0), mg(1), mg(2),
                  pl.BlockSpec((ROW_TILE, D_MODEL), lambda i: (i, 0)),
                  pl.BlockSpec((1, 3, D_MODEL), lambda i: (i // tps, 0, 0)),
                  pl.BlockSpec((3, BR_W, D_MODEL), lambda i: (0, 0, 0)),
                  pl.BlockSpec((D_MODEL, D_MODEL), lambda i: (0, 0))],
        out_specs=pl.BlockSpec((ROW_TILE, D_MODEL), lambda i: (i, 0)),
        compiler_params=_params(("parallel",)),
        name="merge_out",
    )(ya, yb, yc, z, z, z, x, mod, wbr_bf, wout_bf)


def _permute_w_in(w):
    hy_z, hy_g = w[:, 0:1536], w[:, 1536:2048]
    rw_rkv, rw_lora, rw_g = w[:, 2048:3584], w[:, 3584:3840], w[:, 3840:4352]
    da = w[:, 4352:6400]
    mg = w[:, 6400:9472]
    return jnp.concatenate([hy_z, hy_g, rw_rkv, rw_g, da, mg, rw_lora], axis=1)


def _layer(x, mod, p, consts, B, L, lam_init, ctx, bufs, layer):
    tps = L // ROW_TILE
    tabs, zpos, rope_tabs, ones_bd, ones2 = consts
    z = _inproj(x, mod, p["norm_g"], p["w_in_bf"], tps)

    u2, x1g = _hy_pre(z, p["hy_conv_w"], p["hy_conv_b"], tps)
    tk = min(L, 256)
    hfb = _hy_filters(L, zpos, p["hy_f1p"], p["hy_fb1"], p["hy_freq"], p["hy_f2"], p["hy_fb2"], p["hy_f3"])
    hre, him = _filter_dft(L, tk, tabs, hfb)
    spec = _dft_fwd(B, L, tk, tabs, u2, hre, him)
    y_a = _dft_inv(B, L, min(L, 512), tabs, spec, u2, x1g, p["hy_bias"])

    r, nkk, bonus, wf, bf, kdf, wb, bb, kdb, vt = _rw_prep(z, p, ones_bd, tps)
    npair = RW_H // 2
    s0 = jnp.zeros((B, 2, RW_H, RW_N, RW_N), F32) if ctx is None else ctx[2]
    tiles = (B, L // SUB, npair, RW_N, LANES)
    ytf, ytb, sbuf = _rw_scan(B, L, (r, nkk, wf, bf, kdf, wb, bb, kdb), vt.reshape(tiles), s0, ones2,
                              bufs[2], layer)
    flat = (B * L // SUB, npair, RW_N, LANES)
    y_b = _rw_post(ytf.reshape(flat), ytb.reshape(flat), bonus, z, p["rw_ln_w"], p["rw_ln_b"], ones_bd)

    kbuf, vbuf = bufs[:2]
    if ctx is None:
        q2, k_bf, v_bf, kbuf, vbuf = _da_prep(z, p["da_gq"], p["da_gk"], ones_bd, tps, None, (kbuf, vbuf), layer)
        ctx_kv = None
    else:
        q2, k_bf, v_bf = _da_prep(z, p["da_gq"], p["da_gk"], ones_bd, tps, rope_tabs)
        ctx_kv = (ctx[0].reshape(B, -1, BR_W).astype(BF16), ctx[1].reshape(B, -1, BR_W).astype(BF16))
    y_c = _da_attn(B, L, lam_init, q2, k_bf, v_bf, ctx_kv, z, p)

    x_new = _merge(y_a, y_b, y_c, z, x, mod, p["w_br_bf"], p["w_out_bf"], tps)
    return x_new, (kbuf, vbuf, sbuf)


def kernel(x_prompt, x_sample, cache_k, cache_v, state_rwkv, c, c_ctx, norm_g, w_ada, b_ada, w_in, hy_conv_w, hy_conv_b, hy_f1, hy_fb1, hy_freq, hy_f2, hy_fb2, hy_f3, hy_bias, rw_mu, rw_w0, rw_w2, rw_a0, rw_a2, rw_kk, rw_ka, rw_rk, rw_ln_w, rw_ln_b, da_gq, da_gk, da_lq1, da_lk1, da_lq2, da_lk2, da_gsub, w_br, w_out):
    Bp, Lp, _ = x_prompt.shape
    Bs, Ls, _ = x_sample.shape

    pad = (-(Bp + Bs)) % 8
    c_all = jnp.concatenate([jnp.broadcast_to(c_ctx, (Bp, D_MODEL)), c, jnp.zeros((pad, D_MODEL), F32)], axis=0)
    mod_all = _modulation(c_all, w_ada, b_ada)
    mod_p = mod_all[:, :Bp].reshape(DEPTH, Bp, 3, D_MODEL)
    mod_s = mod_all[:, Bp:Bp + Bs].reshape(DEPTH, Bs, 3, D_MODEL)

    seg = jnp.arange(BR_W) // RW_N
    ones_bd = (seg[:, None] == seg[None, :]).astype(BF16)
    ones2 = ones_bd[:2 * LANES, :2 * LANES]

    consts = {}
    for L in sorted({Lp, Ls}):
        consts[L] = (_dft_tables(L), _hy_positions(L), _rope_tables(L), ones_bd, ones2)

    def layer_params(l):
        return dict(norm_g=norm_g[l], w_in_bf=_permute_w_in(w_in[l]).astype(BF16),
                    hy_conv_w=hy_conv_w[l], hy_conv_b=hy_conv_b[l],
                    hy_f1p=jnp.pad(hy_f1[l], ((0, LANES - HY_EMB), (0, 0))), hy_fb1=hy_fb1[l],
                    hy_freq=hy_freq[l], hy_f2=hy_f2[l], hy_fb2=hy_fb2[l], hy_f3=hy_f3[l],
                    hy_bias=hy_bias[l], rw_mu=rw_mu[l], rw_w0=rw_w0[l], rw_w2=rw_w2[l],
                    rw_a0=rw_a0[l], rw_a2=rw_a2[l], rw_kk=rw_kk[l], rw_ka=rw_ka[l], rw_rk=rw_rk[l],
                    rw_ln_w=rw_ln_w[l], rw_ln_b=rw_ln_b[l], da_gq=da_gq[l], da_gk=da_gk[l],
                    da_lq1=da_lq1[l], da_lk1=da_lk1[l], da_lq2=da_lq2[l], da_lk2=da_lk2[l],
                    da_gsub=da_gsub[l], w_br_bf=w_br[l].astype(BF16), w_out_bf=w_out[l].astype(BF16))

    params = [layer_params(l) for l in range(DEPTH)]

    xp = x_prompt.reshape(Bp * Lp, D_MODEL)
    bufs = (jnp.zeros((Bp, DEPTH, Lp, BR_W), F32), jnp.zeros((Bp, DEPTH, Lp, BR_W), F32),
            jnp.zeros((Bp, DEPTH, 2, RW_H, RW_N, RW_N), F32))
    for l in range(DEPTH):
        lam_init = 0.8 - 0.6 * math.exp(-0.3 * l)
        xp, bufs = _layer(xp, mod_p[l], params[l], consts[Lp], Bp, Lp, lam_init, None, bufs, l)
    new_cache_k = bufs[0].reshape(Bp, DEPTH, Lp, DA_H, 2, DA_DH)
    new_cache_v = bufs[1].reshape(Bp, DEPTH, Lp, DA_H, 2 * DA_DH)
    new_state = bufs[2]

    xs = x_sample.reshape(Bs * Ls, D_MODEL)
    scratch_state = jnp.zeros((Bs, 1, 2, RW_H, RW_N, RW_N), F32)
    for l in range(DEPTH):
        lam_init = 0.8 - 0.6 * math.exp(-0.3 * l)
        ctx = (cache_k[:, l], cache_v[:, l], state_rwkv[:, l])
        xs, (_, _, scratch_state) = _layer(xs, mod_s[l], params[l], consts[Ls], Bs, Ls, lam_init, ctx,
                                           (None, None, scratch_state), 0)

    return (xp.reshape(Bp, Lp, D_MODEL), xs.reshape(Bs, Ls, D_MODEL), new_cache_k, new_cache_v, new_state)
```

```python
import functools
import math

import jax
import jax.numpy as jnp
from jax import lax
from jax.experimental import pallas as pl
from jax.experimental.pallas import tpu as pltpu

F32 = jnp.float32
BF16 = jnp.bfloat16

D_MODEL = 1024
DEPTH = 4
GRID_W = 64
BR_W = 512
HY_EMB = 33
HY_BANDS = (HY_EMB - 1) // 2
HY_FFN = 64
HY_MAX_DECAY = math.log(1e-2) / 0.3
HY_MIN_DECAY = math.log(1e-2) / 1.5
RW_N = 64
RW_H = BR_W // RW_N
W_LORA = 64
A_LORA = 64
RW_DECAY_SCALE = math.exp(-0.5)
RW_GN_EPS = 64e-5
DA_DH = 64
DA_H = BR_W // (2 * DA_DH)
ROPE_AXIS = DA_DH // 2
ROPE_BASE = 10000.0
NORM_EPS = 1e-6
N_IN = 9472

LANES = 128
ROW_TILE = 256
MATMUL_ROWS = 512
DFT_ROWS = 1024
HALO = 8
Z_HALO = 16
SCAN_CHUNK = 256
SCAN_BATCH = 2
SCAN_GROUP = 16
SUB = 64
VMEM_LIMIT = 52 * 1024 * 1024

COL_HY_Z = 0
COL_HY_G = 1536
COL_RW_R = 2048
COL_RW_K = 2560
COL_RW_V = 3072
COL_RW_G = 3584
COL_DA_Q = 4096
COL_DA_K = 4608
COL_DA_V = 5120
COL_DA_G = 5632
COL_MG = 6144
COL_LORA = 9216


def _sigmoid(x):
    return 1.0 / (1.0 + jnp.exp(-x))


def _silu(x):
    return x * _sigmoid(x)


def _dot(a, b):
    return jnp.dot(a, b, preferred_element_type=F32)


def _split2(x):
    hi = x.astype(BF16)
    lo = (x - hi.astype(F32)).astype(BF16)
    return hi, lo


def _dot3(a, b):
    ah, al = _split2(a)
    bh, bl = _split2(b)
    return _dot(ah, bh) + _dot(ah, bl) + _dot(al, bh)


def _segsum(x, ones_bd):
    hi, lo = _split2(x)
    return _dot(hi, ones_bd) + _dot(lo, ones_bd)


def _params(sem, vmem=VMEM_LIMIT):
    return pltpu.CompilerParams(dimension_semantics=sem, vmem_limit_bytes=vmem)


def _mod_kernel(c_ref, w_ref, b_ref, o_ref):
    c = c_ref[...]
    o_ref[0] = _dot3(_silu(c), w_ref[0]) + b_ref[0]


def _modulation(c_all, w_ada, b_ada):
    rows = c_all.shape[0]
    nt = 3 * D_MODEL // 1024
    return pl.pallas_call(
        _mod_kernel,
        out_shape=jax.ShapeDtypeStruct((DEPTH, rows, 3 * D_MODEL), F32),
        grid=(DEPTH, nt),
        in_specs=[pl.BlockSpec((rows, D_MODEL), lambda l, j: (0, 0)),
                  pl.BlockSpec((1, D_MODEL, 1024), lambda l, j: (l, 0, j)),
                  pl.BlockSpec((1, 1, 1024), lambda l, j: (l, 0, j))],
        out_specs=pl.BlockSpec((1, rows, 1024), lambda l, j: (l, 0, j)),
        compiler_params=_params(("parallel", "parallel")),
        name="adaln_mod",
    )(c_all, w_ada, b_ada.reshape(DEPTH, 1, 3 * D_MODEL))


def _inproj_kernel(x_ref, mod_ref, g_ref, w_ref, o_ref):
    x = x_ref[...]
    ms = jnp.mean(x * x, axis=-1, keepdims=True)
    y = x * lax.rsqrt(ms + NORM_EPS) * g_ref[...]
    mod = mod_ref[0]
    h = y * (1.0 + mod[1:2]) + mod[0:1]
    o_ref[...] = _dot(h.astype(BF16), w_ref[...]).astype(BF16)


def _inproj(x, mod, norm_g, w_in_bf, L):
    rows = x.shape[0]
    half = N_IN // 2
    tm = min(L, MATMUL_ROWS)
    tps = L // tm
    return pl.pallas_call(
        _inproj_kernel,
        out_shape=jax.ShapeDtypeStruct((rows, N_IN), BF16),
        grid=(2, rows // tm),
        in_specs=[pl.BlockSpec((tm, D_MODEL), lambda j, i: (i, 0)),
                  pl.BlockSpec((1, 3, D_MODEL), lambda j, i: (i // tps, 0, 0)),
                  pl.BlockSpec((1, D_MODEL), lambda j, i: (0, 0)),
                  pl.BlockSpec((D_MODEL, half), lambda j, i: (0, j))],
        out_specs=pl.BlockSpec((tm, half), lambda j, i: (i, j)),
        compiler_params=_params(("parallel", "parallel")),
        name="inproj",
    )(x, mod, norm_g.reshape(1, D_MODEL), w_in_bf)


def _neighbours(cur, prev_blk, next_blk, first, last):
    rows = cur.shape[0]
    rid = lax.broadcasted_iota(jnp.int32, cur.shape, 0)
    prow = jnp.where(first, 0.0, prev_blk[Z_HALO - 1:Z_HALO, :])
    nrow = jnp.where(last, 0.0, next_blk[0:1, :])
    up = jnp.where(rid == 0, prow, pltpu.roll(cur, 1, 0))
    dn = jnp.where(rid == rows - 1, nrow, pltpu.roll(cur, rows - 1, 0))
    return up, dn


def _halo_specs(width, col, rows):
    per = ROW_TILE // Z_HALO
    nblk = rows // Z_HALO
    cb = col // width
    return [pl.BlockSpec((ROW_TILE, width), lambda i: (i, cb)),
            pl.BlockSpec((Z_HALO, width), lambda i: (jnp.maximum(i * per - 1, 0), cb)),
            pl.BlockSpec((Z_HALO, width), lambda i: (jnp.minimum((i + 1) * per, nblk - 1), cb))]


def _hy_pre_kernel(tps, zc, zp, zn, g_ref, cw, cb, u_ref, x1g_ref):
    i = pl.program_id(0)
    first = (i % tps) == 0
    last = (i % tps) == tps - 1
    z = zc[...].astype(F32)
    up, dn = _neighbours(z, zp[...].astype(F32), zn[...].astype(F32), first, last)
    u = up * cw[0:1, :] + z * cw[1:2, :] + dn * cw[2:3, :] + cb[...]
    v = u[:, :BR_W]
    x1 = u[:, BR_W:2 * BR_W]
    x2 = u[:, 2 * BR_W:]
    u2 = x2 * v
    hi, lo = _split2(u2)
    u_ref[:, :BR_W] = hi
    u_ref[:, BR_W:] = lo
    x1g_ref[...] = x1 * _silu(g_ref[...].astype(F32))


def _hy_pre(z, conv_w, conv_b, tps):
    rows = z.shape[0]
    return pl.pallas_call(
        functools.partial(_hy_pre_kernel, tps),
        out_shape=(jax.ShapeDtypeStruct((rows, 2 * BR_W), BF16),
                   jax.ShapeDtypeStruct((rows, BR_W), F32)),
        grid=(rows // ROW_TILE,),
        in_specs=_halo_specs(3 * BR_W, COL_HY_Z, rows) + [
            pl.BlockSpec((ROW_TILE, BR_W), lambda i: (i, COL_HY_G // BR_W)),
            pl.BlockSpec((3, 3 * BR_W), lambda i: (0, 0)),
            pl.BlockSpec((1, 3 * BR_W), lambda i: (0, 0))],
        out_specs=(pl.BlockSpec((ROW_TILE, 2 * BR_W), lambda i: (i, 0)),
                   pl.BlockSpec((ROW_TILE, BR_W), lambda i: (i, 0))),
        compiler_params=_params(("parallel",)),
        name="hy_pre",
    )(z, z, z, z, conv_w, conv_b.reshape(1, 3 * BR_W))


def _filter_kernel(L, zp, f1, fb1, fr, f2, fb2, f3, o_ref):
    freq = fr[...]
    hid = jnp.sin(freq * (_dot3(zp[...], f1[...]) + fb1[...]))
    hid = jnp.sin(freq * (_dot3(hid, f2[...]) + fb2[...]))
    h = _dot3(hid, f3[...])
    row = lax.broadcasted_iota(jnp.int32, (L, BR_W), 0)
    lane = lax.broadcasted_iota(jnp.int32, (L, BR_W), 1)
    t = row.astype(F32) / float(L - 1)
    delta = HY_MIN_DECAY + lane.astype(F32) * ((HY_MAX_DECAY - HY_MIN_DECAY) / float(BR_W - 1))
    win = jnp.exp(-t * jnp.abs(delta))
    hf = h[:, :BR_W] * win
    hb = jnp.where(row == 0, 0.0, h[:, BR_W:] * win)
    o_ref[:, :BR_W] = hf.astype(BF16)
    o_ref[:, BR_W:] = hb.astype(BF16)


def _hy_filters(L, zpos, f1p, fb1, freq, f2, fb2, f3):
    return pl.pallas_call(
        functools.partial(_filter_kernel, L),
        out_shape=jax.ShapeDtypeStruct((L, 2 * BR_W), BF16),
        compiler_params=_params(None),
        name="hy_filter",
    )(zpos, f1p, fb1.reshape(1, HY_FFN), freq.reshape(1, HY_FFN), f2, fb2.reshape(1, HY_FFN), f3)


def _filter_dft_kernel(tk, L, ch, sh, hfb, hre_ref, him_ref):
    j = pl.program_id(0)
    fre = _dot(ch[...], hfb[...])
    fim = _dot(sh[...], hfb[...])
    row = lax.broadcasted_iota(jnp.int32, (tk, BR_W), 0) + j * tk
    is0 = row == 0
    scale = jnp.where(is0, 1.0 / (2 * L), 2.0 / (2 * L))
    hre_ref[...] = (fre[:, :BR_W] + fre[:, BR_W:]) * scale
    him_ref[...] = jnp.where(is0, fim[:, :BR_W] + fim[:, BR_W:], fim[:, :BR_W] - fim[:, BR_W:]) * scale


def _filter_dft(L, tk, tabs, hfb):
    ch, sh, _ = tabs
    tab = pl.BlockSpec((tk, L), lambda j: (j, 0))
    return pl.pallas_call(
        functools.partial(_filter_dft_kernel, tk, L),
        out_shape=(jax.ShapeDtypeStruct((L, BR_W), F32), jax.ShapeDtypeStruct((L, BR_W), F32)),
        grid=(L // tk,),
        in_specs=[tab, tab, pl.BlockSpec((L, 2 * BR_W), lambda j: (0, 0))],
        out_specs=(pl.BlockSpec((tk, BR_W), lambda j: (j, 0)), pl.BlockSpec((tk, BR_W), lambda j: (j, 0))),
        compiler_params=_params(("parallel",)),
        name="hy_filter_dft",
    )(ch, sh, hfb)


def _dft_fwd_kernel(tk, ch, sh, u_ref, hre_ref, him_ref, y_ref):
    j = pl.program_id(1)
    uh = u_ref[:, :BR_W]
    ure = _dot(ch[...], uh)
    uim = _dot(sh[...], uh)
    hre = hre_ref[...]
    him = him_ref[...]
    row = lax.broadcasted_iota(jnp.int32, (tk, BR_W), 0) + j * tk
    is0 = row == 0
    y_ref[:, :BR_W] = jnp.where(is0, ure * hre, ure * hre - uim * him).astype(BF16)
    y_ref[:, BR_W:] = jnp.where(is0, uim * him, ure * him + uim * hre).astype(BF16)


def _dft_fwd(B, L, tk, tabs, u2, hre, him):
    ch, sh, _ = tabs
    nk = L // tk
    tab = pl.BlockSpec((tk, L), lambda b, j: (j, 0))
    hsp = pl.BlockSpec((tk, BR_W), lambda b, j: (j, 0))
    return pl.pallas_call(
        functools.partial(_dft_fwd_kernel, tk),
        out_shape=jax.ShapeDtypeStruct((B * L, 2 * BR_W), BF16),
        grid=(B, nk),
        in_specs=[tab, tab, pl.BlockSpec((L, 2 * BR_W), lambda b, j: (b, 0)), hsp, hsp],
        out_specs=pl.BlockSpec((tk, 2 * BR_W), lambda b, j: (b * nk + j, 0)),
        compiler_params=_params(("parallel", "parallel")),
        name="hy_dft_fwd",
    )(ch, sh, u2, hre, him)


def _dft_inv_kernel(ch, sth, y_ref, u_ref, x1g_ref, bias_ref, o_ref):
    conv = _dot(ch[...], y_ref[:, :BR_W]) + _dot(sth[...], y_ref[:, BR_W:])
    u2 = u_ref[:, :BR_W].astype(F32) + u_ref[:, BR_W:].astype(F32)
    o_ref[...] = (conv + u2 * bias_ref[...]) * x1g_ref[...]


def _dft_inv(B, L, tm, tabs, y2, u2, x1g, bias):
    ch, _, sth = tabs
    nt = L // tm
    tab = pl.BlockSpec((tm, L), lambda b, i: (i, 0))
    return pl.pallas_call(
        _dft_inv_kernel,
        out_shape=jax.ShapeDtypeStruct((B * L, BR_W), F32),
        grid=(B, nt),
        in_specs=[tab, tab,
                  pl.BlockSpec((L, 2 * BR_W), lambda b, i: (b, 0)),
                  pl.BlockSpec((tm, 2 * BR_W), lambda b, i: (b * nt + i, 0)),
                  pl.BlockSpec((tm, BR_W), lambda b, i: (b * nt + i, 0)),
                  pl.BlockSpec((1, BR_W), lambda b, i: (0, 0))],
        out_specs=pl.BlockSpec((tm, BR_W), lambda b, i: (b * nt + i, 0)),
        compiler_params=_params(("parallel", "parallel")),
        name="hy_dft_inv",
    )(ch, sth, y2, u2, x1g, bias.reshape(1, BR_W))


def _dft_tables(L):
    n = 2 * L
    k = jnp.arange(L, dtype=jnp.int32)
    m = (k[:, None] * k[None, :]) % n
    ang = m.astype(F32) * (2.0 * math.pi / n)
    c = jnp.cos(ang)
    s = -jnp.sin(ang)
    alt = jnp.where(k % 2 == 0, 1.0, -1.0).astype(F32)
    s = jnp.where((k == 0)[:, None], alt[None, :], s)
    s = s.astype(BF16)
    return c.astype(BF16), s, s.T


def _hy_positions(L):
    t = jnp.linspace(0.0, 1.0, L, dtype=F32)[:, None]
    w = 2.0 * math.pi * jnp.arange(L, dtype=F32)[:, None] / L
    bands = jnp.linspace(1e-4, HY_BANDS - 1, HY_BANDS, dtype=F32)[None, :]
    z = jnp.concatenate([t, jnp.cos(bands * w), -jnp.sin(bands * w)], axis=-1)
    return jnp.pad(z, ((0, 0), (0, LANES - HY_EMB)))


def _rw_prep_kernel(tps, rc, rp, rn, kc, kp, kn, vc, vp, vn, lc, lp, ln,
                    mu_r, mu_k, mu_v, mu_l, kkw, ka, rk, w0, a0, w2, a2, ones_bd,
                    r_out, nkk_out, bonus_out, wf, bf, kdf, wb, bb, kdb, vt_out):
    i = pl.program_id(0)
    first = (i % tps) == 0
    last = (i % tps) == tps - 1

    def shifted(cur_ref, prev_ref, next_ref, mu_ref):
        cur = cur_ref[...].astype(F32)
        up, dn = _neighbours(cur, prev_ref[...].astype(F32), next_ref[...].astype(F32), first, last)
        return cur + mu_ref[...] * (0.5 * (up + dn) - cur)

    r = shifted(rc, rp, rn, mu_r)
    k = shifted(kc, kp, kn, mu_k)
    v = shifted(vc, vp, vn, mu_v)
    lora = shifted(lc, lp, ln, mu_l)
    ones = ones_bd[...]

    kk = k * kkw[...]
    kk = kk * lax.rsqrt(_segsum(kk * kk, ones) + 1e-12)
    r_out[...] = r
    nkk_out[...] = -kk
    for q in range(ROW_TILE // LANES):
        for p in range(RW_H // 2):
            vt = v[q * LANES:(q + 1) * LANES, p * LANES:(p + 1) * LANES].T
            for s in range(LANES // SUB):
                vt_out[q * (LANES // SUB) + s, p] = jnp.concatenate(
                    [vt[:RW_N, s * SUB:(s + 1) * SUB], vt[RW_N:, s * SUB:(s + 1) * SUB]], axis=1)

    kd_sum = None
    outs = ((wf, bf, kdf), (wb, bb, kdb))
    for d in range(2):
        wl = lora[:, d * W_LORA:(d + 1) * W_LORA]
        al = lora[:, 2 * W_LORA + d * A_LORA:2 * W_LORA + (d + 1) * A_LORA]
        w = jnp.exp(-RW_DECAY_SCALE * _sigmoid(w0[d:d + 1, :] + _dot3(jnp.tanh(wl), w2[d])))
        a = _sigmoid(a0[d:d + 1, :] + _dot3(al, a2[d]))
        kd = k * (1.0 + (a - 1.0) * ka[...])
        outs[d][0][...] = w
        outs[d][1][...] = kk * a
        outs[d][2][...] = kd
        kd_sum = kd if kd_sum is None else kd_sum + kd
    bonus_out[...] = _segsum(r * (0.5 * kd_sum) * rk[...], ones) * v


def _rw_prep(z, p, ones_bd, tps):
    rows = z.shape[0]
    mu = p["rw_mu"]
    vec = lambda a: a.reshape(1, -1)
    full = lambda a: pl.BlockSpec(a.shape, lambda i: (0,) * a.ndim)
    smalls = [vec(mu[0:512]), vec(mu[512:1024]), vec(mu[1024:1536]), vec(mu[1536:1792]),
              vec(p["rw_kk"]), vec(p["rw_ka"]), vec(p["rw_rk"]), p["rw_w0"], p["rw_a0"],
              p["rw_w2"], p["rw_a2"], ones_bd]
    tile = pl.BlockSpec((ROW_TILE, BR_W), lambda i: (i, 0))
    npair = RW_H // 2
    vt_shape = jax.ShapeDtypeStruct((rows // SUB, npair, RW_N, LANES), F32)
    vt_spec = pl.BlockSpec((ROW_TILE // SUB, npair, RW_N, LANES), lambda i: (i, 0, 0, 0))
    return pl.pallas_call(
        functools.partial(_rw_prep_kernel, tps),
        out_shape=tuple(jax.ShapeDtypeStruct((rows, BR_W), F32) for _ in range(9)) + (vt_shape,),
        grid=(rows // ROW_TILE,),
        in_specs=(_halo_specs(BR_W, COL_RW_R, rows) + _halo_specs(BR_W, COL_RW_K, rows)
                  + _halo_specs(BR_W, COL_RW_V, rows) + _halo_specs(2 * LANES, COL_LORA, rows)
                  + [full(a) for a in smalls]),
        out_specs=tuple(tile for _ in range(9)) + (vt_spec,),
        compiler_params=_params(("parallel",)),
        name="rw_prep",
    )(*([z] * 12), *smalls)


def _scan_kernel(nc, bb, r_f, nk_f, w_f, b_f, kd_f, vt_f, r_b, nk_b, w_b, b_b, kd_b, vt_b,
                 s0, ones2, _, yt_f, yt_b, sfin, state):
    c = pl.program_id(1)
    npair = RW_H // 2
    nsub = SCAN_CHUNK // SUB
    chains = [(bi, d, p) for bi in range(bb) for d in range(2) for p in range(npair)]

    @pl.when(c == 0)
    def _():
        for ci, (bi, d, p) in enumerate(chains):
            state[ci] = jnp.concatenate([s0[bi, d, 2 * p], s0[bi, d, 2 * p + 1]], axis=1)

    g4 = ones2[...]
    lane_full = lax.broadcasted_iota(jnp.int32, (RW_N, LANES), 1)
    lane = lane_full & (SUB - 1)
    lane_head = lane_full & SUB
    dirs = ((r_f, nk_f, w_f, b_f, kd_f, vt_f, yt_f), (r_b, nk_b, w_b, b_b, kd_b, vt_b, yt_b))

    for sub in range(nsub):
        subs = (sub, nsub - 1 - sub)
        for bi, d, p in chains:
            dirs[d][6][bi, subs[d], p] = jnp.zeros((RW_N, LANES), F32)

        def body(g, carry):
            bases = (pl.multiple_of(g * HALO, HALO), pl.multiple_of((SUB // HALO - 1 - g) * HALO, HALO))

            for jj in range(HALO):
                js = (jj, HALO - 1 - jj)

                def row(ref, bi, d, p):
                    blk = ref[bi, pl.ds(subs[d] * SUB + bases[d], HALO), pl.ds(p * LANES, LANES)]
                    return blk[js[d]:js[d] + 1, :]

                def reduce_keys(tiles):
                    lhs = jnp.concatenate([jnp.concatenate(tiles[i:i + 2], axis=1) for i in range(0, len(tiles), 2)],
                                          axis=0)
                    res = _dot(lhs, g4)
                    return [res[(i // 2) * RW_N:(i // 2 + 1) * RW_N, (i % 2) * LANES:(i % 2 + 1) * LANES]
                            for i in range(len(tiles))]

                for g0 in range(0, len(chains), SCAN_GROUP):
                    group = list(enumerate(chains))[g0:g0 + SCAN_GROUP]
                    sas = reduce_keys([state[ci].astype(BF16) * row(dirs[d][1], bi, d, p).astype(BF16)
                                       for ci, (bi, d, p) in group])
                    zr = []
                    for (ci, (bi, d, p)), sa in zip(group, sas):
                        tt = bases[d] + js[d]
                        vb = jnp.take_along_axis(dirs[d][5][bi, subs[d], p], lane_head + tt, axis=1)
                        st = (state[ci] * row(dirs[d][2], bi, d, p) + sa * row(dirs[d][3], bi, d, p)
                              + vb * row(dirs[d][4], bi, d, p))
                        state[ci] = st
                        zr.append(st.astype(BF16) * row(dirs[d][0], bi, d, p).astype(BF16))
                    for (ci, (bi, d, p)), yb in zip(group, reduce_keys(zr)):
                        y_ref = dirs[d][6]
                        sel = lane == bases[d] + js[d]
                        y_ref[bi, subs[d], p] = jnp.where(sel, yb, y_ref[bi, subs[d], p])
            return carry

        lax.fori_loop(0, SUB // HALO, body, 0)

    @pl.when(c == nc - 1)
    def _():
        for ci, (bi, d, p) in enumerate(chains):
            sfin[bi, 0, d, 2 * p] = state[ci][:, :RW_N]
            sfin[bi, 0, d, 2 * p + 1] = state[ci][:, RW_N:]


def _rw_scan(B, L, seqs, vt, s0, ones2, sbuf, layer):
    r, nkk, wf, bf, kdf, wb, bwb, kdb = [a.reshape(B, L, BR_W) for a in seqs]
    bb = SCAN_BATCH
    nc = L // SCAN_CHUNK
    nsub = SCAN_CHUNK // SUB
    npair = RW_H // 2
    nch = bb * 2 * npair
    fwd = pl.BlockSpec((bb, SCAN_CHUNK, BR_W), lambda b, c: (b, c, 0))
    bwd = pl.BlockSpec((bb, SCAN_CHUNK, BR_W), lambda b, c: (b, nc - 1 - c, 0))
    tile = (bb, nsub, npair, RW_N, LANES)
    vfwd = pl.BlockSpec(tile, lambda b, c: (b, c, 0, 0, 0))
    vbwd = pl.BlockSpec(tile, lambda b, c: (b, nc - 1 - c, 0, 0, 0))
    sspec = pl.BlockSpec((bb, 2, RW_H, RW_N, RW_N), lambda b, c: (b, 0, 0, 0, 0))
    yshape = jax.ShapeDtypeStruct((B, L // SUB, npair, RW_N, LANES), F32)
    fspec = pl.BlockSpec((bb, 1, 2, RW_H, RW_N, RW_N), lambda b, c: (b, layer, 0, 0, 0, 0))
    return pl.pallas_call(
        functools.partial(_scan_kernel, nc, bb),
        out_shape=(yshape, yshape, jax.ShapeDtypeStruct(sbuf.shape, F32)),
        grid=(B // bb, nc),
        in_specs=[fwd, fwd, fwd, fwd, fwd, vfwd, bwd, bwd, bwd, bwd, bwd, vbwd, sspec,
                  pl.BlockSpec((2 * LANES, 2 * LANES), lambda b, c: (0, 0)),
                  pl.BlockSpec(memory_space=pl.ANY)],
        out_specs=(vfwd, vbwd, fspec),
        scratch_shapes=[pltpu.VMEM((nch, RW_N, LANES), F32)],
        input_output_aliases={14: 2},
        compiler_params=_params(("parallel", "arbitrary")),
        name="rw_scan",
    )(r, nkk, wf, bf, kdf, vt, r, nkk, wb, bwb, kdb, vt, s0, ones2, sbuf)


def _rw_post_kernel(yf, yb, bonus, g_ref, lnw, lnb, ones_bd, o_ref, y_sc):
    ones = ones_bd[...]
    nsub = LANES // SUB
    for q in range(ROW_TILE // LANES):
        for p in range(RW_H // 2):
            tiles = [yf[q * nsub + s, p] + yb[q * nsub + s, p] for s in range(nsub)]
            heads = [jnp.concatenate([t[:, h * SUB:(h + 1) * SUB] for t in tiles], axis=1) for h in range(2)]
            y_sc[q * LANES:(q + 1) * LANES, p * LANES:(p + 1) * LANES] = jnp.concatenate(heads, axis=0).T
    y = y_sc[...]
    mu = _segsum(y, ones) * (1.0 / RW_N)
    yc = y - mu
    var = _segsum(yc * yc, ones) * (1.0 / RW_N)
    y = yc * lax.rsqrt(var + RW_GN_EPS) * lnw[...] + lnb[...]
    o_ref[...] = (y + bonus[...]) * _silu(g_ref[...].astype(F32))


def _rw_post(ytf, ytb, bonus, z, ln_w, ln_b, ones_bd):
    rows = bonus.shape[0]
    npair = RW_H // 2
    tile = pl.BlockSpec((ROW_TILE, BR_W), lambda i: (i, 0))
    ytile = pl.BlockSpec((ROW_TILE // SUB, npair, RW_N, LANES), lambda i: (i, 0, 0, 0))
    vec = pl.BlockSpec((1, BR_W), lambda i: (0, 0))
    return pl.pallas_call(
        _rw_post_kernel,
        out_shape=jax.ShapeDtypeStruct((rows, BR_W), F32),
        grid=(rows // ROW_TILE,),
        in_specs=[ytile, ytile, tile, pl.BlockSpec((ROW_TILE, BR_W), lambda i: (i, COL_RW_G // BR_W)),
                  vec, vec, pl.BlockSpec((BR_W, BR_W), lambda i: (0, 0))],
        out_specs=tile,
        scratch_shapes=[pltpu.VMEM((ROW_TILE, BR_W), F32)],
        compiler_params=_params(("parallel",)),
        name="rw_post",
    )(ytf, ytb, bonus, z, ln_w.reshape(1, BR_W), ln_b.reshape(1, BR_W), ones_bd)


def _rope(x, cos, sin_signed):
    lane = lax.broadcasted_iota(jnp.int32, x.shape, 1)
    low = (lane & (ROPE_AXIS - 1)) < (ROPE_AXIS // 2)
    width = x.shape[1]
    partner = jnp.where(low, pltpu.roll(x, width - ROPE_AXIS // 2, 1), pltpu.roll(x, ROPE_AXIS // 2, 1))
    return x * cos + partner * sin_signed


def _da_prep_kernel(rope, q_ref, k_ref, v_ref, gq, gk, ones_bd, *rest):
    if rope:
        cos_ref, sin_ref, q_out, k_out, v_out = rest
    else:
        _, _, q_out, k_out, v_out, kn_out, vc_out = rest
        vc_out[0, 0] = v_ref[...].astype(F32)
    ones = ones_bd[...]
    v_out[...] = v_ref[...]

    def norm(x, g):
        ms = _segsum(x * x, ones) * (1.0 / DA_DH)
        return x * lax.rsqrt(ms + NORM_EPS) * g

    qn = norm(q_ref[...].astype(F32), gq[...])
    kn = norm(k_ref[...].astype(F32), gk[...])
    if rope:
        qn = _rope(qn, cos_ref[...], sin_ref[...])
        kn = _rope(kn, cos_ref[...], sin_ref[...])
    else:
        kn_out[0, 0] = kn
    k_out[...] = kn.astype(BF16)
    qn = qn * (DA_DH ** -0.5)
    lane = lax.broadcasted_iota(jnp.int32, (qn.shape[0], LANES), 1)
    for h in range(DA_H):
        qh = qn[:, h * LANES:(h + 1) * LANES]
        q_out[:, (2 * h) * LANES:(2 * h + 1) * LANES] = jnp.where(lane < DA_DH, qh, 0.0).astype(BF16)
        q_out[:, (2 * h + 1) * LANES:(2 * h + 2) * LANES] = jnp.where(lane >= DA_DH, qh, 0.0).astype(BF16)


def _da_prep(z, gq, gk, ones_bd, tps, rope_tabs, caches=None, layer=0):
    rows = z.shape[0]
    rope = rope_tabs is not None
    aliases = {}
    tile = lambda col: pl.BlockSpec((ROW_TILE, BR_W), lambda i: (i, col // BR_W))
    vec = pl.BlockSpec((1, BR_W), lambda i: (0, 0))
    in_specs = [tile(COL_DA_Q), tile(COL_DA_K), tile(COL_DA_V), vec, vec,
                pl.BlockSpec((BR_W, BR_W), lambda i: (0, 0))]
    args = [z, z, z, jnp.tile(gq, 2 * DA_H).reshape(1, BR_W), jnp.tile(gk, 2 * DA_H).reshape(1, BR_W), ones_bd]
    out_tile = pl.BlockSpec((ROW_TILE, BR_W), lambda i: (i, 0))
    out_shape = [jax.ShapeDtypeStruct((rows, 2 * BR_W), BF16), jax.ShapeDtypeStruct((rows, BR_W), BF16),
                 jax.ShapeDtypeStruct((rows, BR_W), BF16)]
    out_specs = [pl.BlockSpec((ROW_TILE, 2 * BR_W), lambda i: (i, 0)), out_tile, out_tile]
    if rope:
        tab = pl.BlockSpec((ROW_TILE, BR_W), lambda i: (i % tps, 0))
        in_specs += [tab, tab]
        args += list(rope_tabs)
    else:
        cspec = pl.BlockSpec((1, 1, ROW_TILE, BR_W), lambda i: (i // tps, layer, i % tps, 0))
        for buf in caches:
            aliases[len(args)] = len(out_shape)
            in_specs.append(pl.BlockSpec(memory_space=pl.ANY))
            args.append(buf)
            out_shape.append(jax.ShapeDtypeStruct(buf.shape, buf.dtype))
            out_specs.append(cspec)
    return pl.pallas_call(
        functools.partial(_da_prep_kernel, rope),
        out_shape=tuple(out_shape),
        grid=(rows // ROW_TILE,),
        in_specs=in_specs,
        out_specs=tuple(out_specs),
        input_output_aliases=aliases,
        compiler_params=_params(("parallel",)),
        name="da_prep",
    )(*args)


def _rope_tables(L):
    half = ROPE_AXIS // 2
    lane = jnp.arange(BR_W)
    j = lane % DA_DH
    use_col = (j // ROPE_AXIS) == 1
    idx = (j % ROPE_AXIS) % half
    freqs = ROPE_BASE ** (-idx.astype(F32) / half)
    t = jnp.arange(L)
    pos = jnp.where(use_col[None, :], (t % GRID_W)[:, None], (t // GRID_W)[:, None]).astype(F32)
    ang = pos * freqs[None, :]
    sign = jnp.where((j % ROPE_AXIS) < half, -1.0, 1.0).astype(F32)
    return jnp.cos(ang), jnp.sin(ang) * sign[None, :]


def _da_attn_kernel(lam_init, has_ctx, q_ref, k_ref, v_ref, *rest):
    if has_ctx:
        kc_ref, vc_ref, g_ref, gsub, lq1, lk1, lq2, lk2, o_ref = rest
    else:
        g_ref, gsub, lq1, lk1, lq2, lk2, o_ref = rest
    lam = (jnp.exp(jnp.sum(lq1[...] * lk1[...], axis=-1, keepdims=True))
           - jnp.exp(jnp.sum(lq2[...] * lk2[...], axis=-1, keepdims=True)) + lam_init)
    nt = (((1,), (1,)), ((), ()))
    for h in range(DA_H):
        cols = slice(h * LANES, (h + 1) * LANES)
        parts = [(k_ref[:, cols], v_ref[:, cols])]
        if has_ctx:
            parts.append((kc_ref[0, :, cols], vc_ref[0, :, cols]))
        outs = []
        for m in range(2):
            q = q_ref[:, (2 * h + m) * LANES:(2 * h + m + 1) * LANES]
            scores = [lax.dot_general(q, kh, nt, preferred_element_type=F32) for kh, _ in parts]
            top = functools.reduce(jnp.maximum, [jnp.max(sc, axis=-1, keepdims=True) for sc in scores])
            num, den = 0.0, 0.0
            for sc, (_, vh) in zip(scores, parts):
                e = jnp.exp(sc - top)
                den = den + jnp.sum(e, axis=-1, keepdims=True)
                num = num + _dot(e.astype(BF16), vh)
            outs.append(num / den)
        o = outs[0] - lam * outs[1]
        o = o * lax.rsqrt(jnp.mean(o * o, axis=-1, keepdims=True) + 1e-5) * gsub[...] * (1.0 - lam_init)
        o_ref[:, cols] = o * _silu(g_ref[:, cols].astype(F32))


def _da_attn(B, L, lam_init, q2, k_bf, v_bf, ctx_kv, z, p):
    tq = min(L, MATMUL_ROWS)
    nt = L // tq
    vec = lambda a: a.reshape(1, -1)
    small = lambda n: pl.BlockSpec((1, n), lambda b, i: (0, 0))
    seq = pl.BlockSpec((L, BR_W), lambda b, i: (b, 0))
    in_specs = [pl.BlockSpec((tq, 2 * BR_W), lambda b, i: (b * nt + i, 0)), seq, seq]
    args = [q2, k_bf, v_bf]
    if ctx_kv is not None:
        past = ctx_kv[0].shape[1]
        cspec = pl.BlockSpec((1, past, BR_W), lambda b, i: (b, 0, 0))
        in_specs += [cspec, cspec]
        args += list(ctx_kv)
    in_specs += [pl.BlockSpec((tq, BR_W), lambda b, i: (b * nt + i, COL_DA_G // BR_W)),
                 small(2 * DA_DH), small(DA_DH), small(DA_DH), small(DA_DH), small(DA_DH)]
    args += [z, vec(p["da_gsub"]), vec(p["da_lq1"]), vec(p["da_lk1"]), vec(p["da_lq2"]), vec(p["da_lk2"])]
    return pl.pallas_call(
        functools.partial(_da_attn_kernel, lam_init, ctx_kv is not None),
        out_shape=jax.ShapeDtypeStruct((B * L, BR_W), F32),
        grid=(B, nt),
        in_specs=in_specs,
        out_specs=pl.BlockSpec((tq, BR_W), lambda b, i: (b * nt + i, 0)),
        compiler_params=_params(("parallel", "parallel")),
        name="da_attn",
    )(*args)


def _merge_kernel(ya, yb, yc, mga, mgb, mgc, x_ref, mod_ref, wbr, wout, o_ref):
    merged = (_sigmoid(mga[...].astype(F32)) * _dot(ya[...].astype(BF16), wbr[0])
              + _sigmoid(mgb[...].astype(F32)) * _dot(yb[...].astype(BF16), wbr[1])
              + _sigmoid(mgc[...].astype(F32)) * _dot(yc[...].astype(BF16), wbr[2]))
    out = _dot(merged.astype(BF16), wout[...])
    o_ref[...] = x_ref[...] + mod_ref[0][2:3] * out


def _merge(ya, yb, yc, z, x, mod, wbr_bf, wout_bf, tps):
    rows = x.shape[0]
    tile = pl.BlockSpec((ROW_TILE, BR_W), lambda i: (i, 0))
    mg = lambda n: pl.BlockSpec((ROW_TILE, D_MODEL), lambda i: (i, COL_MG // D_MODEL + n))
    return pl.pallas_call(
        _merge_kernel,
        out_shape=jax.ShapeDtypeStruct((rows, D_MODEL), F32),
        grid=(rows // ROW_TILE,),
        in_specs=[tile, tile, tile, mg(0), mg(1), mg(2),
                  pl.BlockSpec((ROW_TILE, D_MODEL), lambda i: (i, 0)),
                  pl.BlockSpec((1, 3, D_MODEL), lambda i: (i // tps, 0, 0)),
                  pl.BlockSpec((3, BR_W, D_MODEL), lambda i: (0, 0, 0)),
                  pl.BlockSpec((D_MODEL, D_MODEL), lambda i: (0, 0))],
        out_specs=pl.BlockSpec((ROW_TILE, D_MODEL), lambda i: (i, 0)),
        compiler_params=_params(("parallel",)),
        name="merge_out",
    )(ya, yb, yc, z, z, z, x, mod, wbr_bf, wout_bf)


def _permute_w_in(w):
    hy_z, hy_g = w[:, 0:1536], w[:, 1536:2048]
    rw_rkv, rw_lora, rw_g = w[:, 2048:3584], w[:, 3584:3840], w[:, 3840:4352]
    da = w[:, 4352:6400]
    mg = w[:, 6400:9472]
    return jnp.concatenate([hy_z, hy_g, rw_rkv, rw_g, da, mg, rw_lora], axis=1)


def _layer(x, mod, p, consts, B, L, lam_init, ctx, bufs, layer):
    tps = L // ROW_TILE
    tabs, zpos, rope_tabs, ones_bd, ones2 = consts
    z = _inproj(x, mod, p["norm_g"], p["w_in_bf"], L)

    u2, x1g = _hy_pre(z, p["hy_conv_w"], p["hy_conv_b"], tps)
    tk = min(L, DFT_ROWS)
    hfb = _hy_filters(L, zpos, p["hy_f1p"], p["hy_fb1"], p["hy_freq"], p["hy_f2"], p["hy_fb2"], p["hy_f3"])
    hre, him = _filter_dft(L, tk, tabs, hfb)
    spec = _dft_fwd(B, L, tk, tabs, u2, hre, him)
    y_a = _dft_inv(B, L, tk, tabs, spec, u2, x1g, p["hy_bias"])

    r, nkk, bonus, wf, bf, kdf, wb, bb, kdb, vt = _rw_prep(z, p, ones_bd, tps)
    npair = RW_H // 2
    s0 = jnp.zeros((B, 2, RW_H, RW_N, RW_N), F32) if ctx is None else ctx[2]
    tiles = (B, L // SUB, npair, RW_N, LANES)
    ytf, ytb, sbuf = _rw_scan(B, L, (r, nkk, wf, bf, kdf, wb, bb, kdb), vt.reshape(tiles), s0, ones2,
                              bufs[2], layer)
    flat = (B * L // SUB, npair, RW_N, LANES)
    y_b = _rw_post(ytf.reshape(flat), ytb.reshape(flat), bonus, z, p["rw_ln_w"], p["rw_ln_b"], ones_bd)

    kbuf, vbuf = bufs[:2]
    if ctx is None:
        q2, k_bf, v_bf, kbuf, vbuf = _da_prep(z, p["da_gq"], p["da_gk"], ones_bd, tps, None, (kbuf, vbuf), layer)
        ctx_kv = None
    else:
        q2, k_bf, v_bf = _da_prep(z, p["da_gq"], p["da_gk"], ones_bd, tps, rope_tabs)
        ctx_kv = (ctx[0].reshape(B, -1, BR_W).astype(BF16), ctx[1].reshape(B, -1, BR_W).astype(BF16))
    y_c = _da_attn(B, L, lam_init, q2, k_bf, v_bf, ctx_kv, z, p)

    x_new = _merge(y_a, y_b, y_c, z, x, mod, p["w_br_bf"], p["w_out_bf"], tps)
    return x_new, (kbuf, vbuf, sbuf)


def kernel(x_prompt, x_sample, cache_k, cache_v, state_rwkv, c, c_ctx, norm_g, w_ada, b_ada, w_in, hy_conv_w, hy_conv_b, hy_f1, hy_fb1, hy_freq, hy_f2, hy_fb2, hy_f3, hy_bias, rw_mu, rw_w0, rw_w2, rw_a0, rw_a2, rw_kk, rw_ka, rw_rk, rw_ln_w, rw_ln_b, da_gq, da_gk, da_lq1, da_lk1, da_lq2, da_lk2, da_gsub, w_br, w_out):
    Bp, Lp, _ = x_prompt.shape
    Bs, Ls, _ = x_sample.shape

    pad = (-(Bp + Bs)) % 8
    c_all = jnp.concatenate([jnp.broadcast_to(c_ctx, (Bp, D_MODEL)), c, jnp.zeros((pad, D_MODEL), F32)], axis=0)
    mod_all = _modulation(c_all, w_ada, b_ada)
    mod_p = mod_all[:, :Bp].reshape(DEPTH, Bp, 3, D_MODEL)
    mod_s = mod_all[:, Bp:Bp + Bs].reshape(DEPTH, Bs, 3, D_MODEL)

    seg = jnp.arange(BR_W) // RW_N
    ones_bd = (seg[:, None] == seg[None, :]).astype(BF16)
    ones2 = ones_bd[:2 * LANES, :2 * LANES]

    consts = {}
    for L in sorted({Lp, Ls}):
        consts[L] = (_dft_tables(L), _hy_positions(L), _rope_tables(L), ones_bd, ones2)

    def layer_params(l):
        return dict(norm_g=norm_g[l], w_in_bf=_permute_w_in(w_in[l]).astype(BF16),
                    hy_conv_w=hy_conv_w[l], hy_conv_b=hy_conv_b[l],
                    hy_f1p=jnp.pad(hy_f1[l], ((0, LANES - HY_EMB), (0, 0))), hy_fb1=hy_fb1[l],
                    hy_freq=hy_freq[l], hy_f2=hy_f2[l], hy_fb2=hy_fb2[l], hy_f3=hy_f3[l],
                    hy_bias=hy_bias[l], rw_mu=rw_mu[l], rw_w0=rw_w0[l], rw_w2=rw_w2[l],
                    rw_a0=rw_a0[l], rw_a2=rw_a2[l], rw_kk=rw_kk[l], rw_ka=rw_ka[l], rw_rk=rw_rk[l],
                    rw_ln_w=rw_ln_w[l], rw_ln_b=rw_ln_b[l], da_gq=da_gq[l], da_gk=da_gk[l],
                    da_lq1=da_lq1[l], da_lk1=da_lk1[l], da_lq2=da_lq2[l], da_lk2=da_lk2[l],
                    da_gsub=da_gsub[l], w_br_bf=w_br[l].astype(BF16), w_out_bf=w_out[l].astype(BF16))

    params = [layer_params(l) for l in range(DEPTH)]

    xp = x_prompt.reshape(Bp * Lp, D_MODEL)
    bufs = (jnp.zeros((Bp, DEPTH, Lp, BR_W), F32), jnp.zeros((Bp, DEPTH, Lp, BR_W), F32),
            jnp.zeros((Bp, DEPTH, 2, RW_H, RW_N, RW_N), F32))
    for l in range(DEPTH):
        lam_init = 0.8 - 0.6 * math.exp(-0.3 * l)
        xp, bufs = _layer(xp, mod_p[l], params[l], consts[Lp], Bp, Lp, lam_init, None, bufs, l)
    new_cache_k = bufs[0].reshape(Bp, DEPTH, Lp, DA_H, 2, DA_DH)
    new_cache_v = bufs[1].reshape(Bp, DEPTH, Lp, DA_H, 2 * DA_DH)
    new_state = bufs[2]

    xs = x_sample.reshape(Bs * Ls, D_MODEL)
    scratch_state = jnp.zeros((Bs, 1, 2, RW_H, RW_N, RW_N), F32)
    for l in range(DEPTH):
        lam_init = 0.8 - 0.6 * math.exp(-0.3 * l)
        ctx = (cache_k[:, l], cache_v[:, l], state_rwkv[:, l])
        xs, (_, _, scratch_state) = _layer(xs, mod_s[l], params[l], consts[Ls], Bs, Ls, lam_init, ctx,
                                           (None, None, scratch_state), 0)

    return (xp.reshape(Bp, Lp, D_MODEL), xs.reshape(Bs, Ls, D_MODEL), new_cache_k, new_cache_v, new_state)
```
